```python
import jax, jax.numpy as jnp
from jax import lax
import numpy as np

D_MODEL = 1024
BATCH = 8
SEQ = 4096
DEPTH = 1

D_PLE = 256
D_FF = 2816
NH_M = 4
DV_M = D_MODEL // 8
DK_M = DV_M // 2
CHUNK_M = 64
CONV_W = 4
NH_F = 8
DH_F = D_MODEL // 16
Q_BLOCK = 128
D_MIX = NH_M * DV_M + NH_F * DH_F
IN_SIZES = (2 * NH_M * DK_M, NH_M * DV_M, NH_M * DV_M, 2 * NH_M,
            NH_F * DH_F, NH_F * DH_F, NH_F * DH_F, NH_F)
N_IN = sum(IN_SIZES)
EPS = 1e-6
MLSTM_F_BIAS = 3.0
FOX_F_BIAS = 2.0

kernel_name = "hymba_mlstm_fox_macaron"


def rmsnorm(x, g):
    xf = x.astype(jnp.float32)
    y = xf * lax.rsqrt(jnp.mean(xf * xf, axis=-1, keepdims=True) + EPS)
    return (y * g.astype(jnp.float32)).astype(x.dtype)


def swiglu(x, w_gate, w_up, w_down):
    return (jax.nn.silu(x @ w_gate) * (x @ w_up)) @ w_down


def causal_depthwise_conv(x, w):
    return lax.conv_general_dilated(
        x, w[:, None, :].astype(x.dtype), window_strides=(1,),
        padding=[(w.shape[0] - 1, 0)], dimension_numbers=("NWC", "WIO", "NWC"),
        feature_group_count=x.shape[-1])


def head_rmsnorm(t, g):
    B, S, NH, DH = t.shape
    return rmsnorm(t, g.reshape(NH, DH)).reshape(B, S, NH * DH)


def mlstm_chunkwise(q, k, v, i_pre, f_pre):
    B, NH, S, DK = q.shape
    DV = v.shape[-1]
    L = CHUNK_M
    NC = S // L
    q = q.reshape(B, NH, NC, L, DK) * DK ** -0.5
    k = k.reshape(B, NH, NC, L, DK)
    v = v.reshape(B, NH, NC, L, DV)
    log_i = i_pre.reshape(B, NH, NC, L)
    log_f = jax.nn.log_sigmoid(f_pre).reshape(B, NH, NC, L)
    b = jnp.cumsum(log_f, axis=-1)
    g = b[..., -1]
    a = g[..., None] - b + log_i

    def step(carry, xs):
        C, n, m = carry
        k_c, v_c, a_c, g_c = xs
        m_new = jnp.maximum(g_c + m, jnp.max(a_c, axis=-1))
        decay = jnp.exp(g_c + m - m_new)
        w = jnp.exp(a_c - m_new[..., None])
        C_new = decay[..., None, None] * C + jnp.einsum("bhl,bhld,bhle->bhde", w, k_c, v_c)
        n_new = decay[..., None] * n + jnp.einsum("bhl,bhld->bhd", w, k_c)
        return (C_new, n_new, m_new), (C, n, m)

    init = (jnp.zeros((B, NH, DK, DV), jnp.float32), jnp.zeros((B, NH, DK), jnp.float32),
            jnp.zeros((B, NH), jnp.float32))
    xs = (jnp.moveaxis(k, 2, 0), jnp.moveaxis(v, 2, 0), jnp.moveaxis(a, 2, 0), jnp.moveaxis(g, 2, 0))
    _, (C_prev, n_prev, m_prev) = lax.scan(step, init, xs)
    C_prev = jnp.moveaxis(C_prev, 0, 2)
    n_prev = jnp.moveaxis(n_prev, 0, 2)
    m_prev = jnp.moveaxis(m_prev, 0, 2)

    inter = b + m_prev[..., None]
    causal = jnp.tril(jnp.ones((L, L), dtype=bool))
    D = jnp.where(causal, b[..., :, None] - b[..., None, :] + log_i[..., None, :], -jnp.inf)
    m_t = jnp.maximum(inter, jnp.max(D, axis=-1))
    w_inter = jnp.exp(inter - m_t)
    P = jnp.exp(D - m_t[..., None]) * jnp.einsum("bhcld,bhcsd->bhcls", q, k)
    num = (w_inter[..., None] * jnp.einsum("bhcld,bhcde->bhcle", q, C_prev)
           + jnp.einsum("bhcls,bhcse->bhcle", P, v))
    den = w_inter * jnp.einsum("bhcld,bhcd->bhcl", q, n_prev) + jnp.sum(P, axis=-1)
    h = num / jnp.maximum(jnp.abs(den), jnp.exp(-m_t))[..., None]
    return h.reshape(B, NH, S, DV)


def forgetting_attention(q, k, v, f_pre):
    S = q.shape[2]
    scale = q.shape[-1] ** -0.5
    c = jnp.cumsum(jax.nn.log_sigmoid(f_pre.astype(jnp.float32)), axis=-1)
    outs = []
    for blk in range(S // Q_BLOCK):
        q0, q1 = blk * Q_BLOCK, (blk + 1) * Q_BLOCK
        logits = jnp.einsum("bhqd,bhkd->bhqk", q[:, :, q0:q1], k[:, :, :q1]).astype(jnp.float32) * scale
        logits = logits + c[:, :, q0:q1, None] - c[:, :, None, :q1]
        mask = (q0 + jnp.arange(Q_BLOCK))[:, None] >= jnp.arange(q1)[None, :]
        probs = jax.nn.softmax(jnp.where(mask, logits, -jnp.inf), axis=-1).astype(v.dtype)
        outs.append(jnp.einsum("bhqk,bhkd->bhqd", probs, v[:, :, :q1]))
    return jnp.concatenate(outs, axis=2)


def setup_inputs(seed: int = 0) -> dict:
    key = jax.random.key(seed)
    ks = jax.random.split(key, 24)
    f32 = jnp.float32

    def nrm(k, shape, fan_in):
        return jax.random.normal(k, shape, f32) * fan_in ** -0.5

    def gain(k, n):
        return 1.0 + 0.02 * jax.random.normal(k, (DEPTH, n), f32)

    b_gates = jnp.concatenate(
        [0.1 * jax.random.normal(ks[9], (DEPTH, NH_M), f32),
         MLSTM_F_BIAS + 0.1 * jax.random.normal(ks[10], (DEPTH, NH_M), f32)], axis=-1)
    return {
        "x": jax.random.normal(ks[0], (BATCH, SEQ, D_MODEL), f32),
        "p": jax.random.normal(ks[1], (DEPTH, BATCH, SEQ, D_PLE), f32),
        "ffn1_norm": gain(ks[2], D_MODEL),
        "ffn1_w_gate": nrm(ks[3], (DEPTH, D_MODEL, D_FF), D_MODEL),
        "ffn1_w_up": nrm(ks[4], (DEPTH, D_MODEL, D_FF), D_MODEL),
        "ffn1_w_down": nrm(ks[5], (DEPTH, D_FF, D_MODEL), D_FF),
        "mix_norm": gain(ks[6], D_MODEL),
        "w_in": nrm(ks[7], (DEPTH, D_MODEL, N_IN), D_MODEL),
        "conv_qk": nrm(ks[8], (DEPTH, CONV_W, 2 * NH_M * DK_M), CONV_W),
        "b_mlstm_gates": b_gates,
        "b_fox_f": FOX_F_BIAS + 0.1 * jax.random.normal(ks[11], (DEPTH, NH_F), f32),
        "mlstm_out_norm": gain(ks[12], NH_M * DV_M),
        "fox_out_norm": gain(ks[13], NH_F * DH_F),
        "w_out": nrm(ks[14], (DEPTH, D_MIX, D_MODEL), D_MIX),
        "ffn2_norm": gain(ks[15], D_MODEL),
        "ffn2_w_gate": nrm(ks[16], (DEPTH, D_MODEL, D_FF), D_MODEL),
        "ffn2_w_up": nrm(ks[17], (DEPTH, D_MODEL, D_FF), D_MODEL),
        "ffn2_w_down": nrm(ks[18], (DEPTH, D_FF, D_MODEL), D_FF),
        "ple_gate_norm": gain(ks[19], D_MODEL),
        "w_ple_gate": nrm(ks[20], (DEPTH, D_MODEL, D_MODEL), D_MODEL),
        "w_ple_proj": nrm(ks[21], (DEPTH, D_PLE, D_MODEL), D_PLE),
        "ple_proj_norm": gain(ks[22], D_MODEL),
        "final_norm": 1.0 + 0.02 * jax.random.normal(ks[23], (D_MODEL,), f32),
    }


def reference(x, p, ffn1_norm, ffn1_w_gate, ffn1_w_up, ffn1_w_down, mix_norm, w_in, conv_qk,
              b_mlstm_gates, b_fox_f, mlstm_out_norm, fox_out_norm, w_out, ffn2_norm,
              ffn2_w_gate, ffn2_w_up, ffn2_w_down, ple_gate_norm, w_ple_gate, w_ple_proj,
              ple_proj_norm, final_norm):
    B, S, _ = x.shape
    split_idx = [int(s) for s in np.cumsum(IN_SIZES)[:-1]]

    def heads(t, nh):
        return t.reshape(B, S, nh, -1).transpose(0, 2, 1, 3)

    h = x
    for i in range(DEPTH):
        h = h + 0.5 * swiglu(rmsnorm(h, ffn1_norm[i]), ffn1_w_gate[i], ffn1_w_up[i], ffn1_w_down[i])

        u = rmsnorm(h, mix_norm[i])
        z = u @ w_in[i]
        m_qk, m_v, m_o, m_if, f_q, f_k, f_v, f_f = jnp.split(z, split_idx, axis=-1)

        m_qk = jax.nn.silu(causal_depthwise_conv(m_qk, conv_qk[i]))
        m_q, m_k = jnp.split(m_qk, 2, axis=-1)
        gates = (m_if.astype(jnp.float32) + b_mlstm_gates[i].astype(jnp.float32)).transpose(0, 2, 1)
        h_m = mlstm_chunkwise(heads(m_q, NH_M).astype(jnp.float32), heads(m_k, NH_M).astype(jnp.float32),
                              heads(m_v, NH_M).astype(jnp.float32), gates[:, :NH_M], gates[:, NH_M:])
        h_m = h_m.astype(u.dtype).transpose(0, 2, 1, 3)
        y_m = head_rmsnorm(h_m, mlstm_out_norm[i]) * jax.nn.sigmoid(m_o)

        f_gate = (f_f + b_fox_f[i]).transpose(0, 2, 1)
        h_f = forgetting_attention(heads(f_q, NH_F), heads(f_k, NH_F), heads(f_v, NH_F), f_gate)
        y_f = head_rmsnorm(h_f.transpose(0, 2, 1, 3), fox_out_norm[i])

        h = h + jnp.concatenate([y_m, y_f], axis=-1) @ w_out[i]

        h = h + 0.5 * swiglu(rmsnorm(h, ffn2_norm[i]), ffn2_w_gate[i], ffn2_w_up[i], ffn2_w_down[i])

        gate = jax.nn.sigmoid(rmsnorm(h, ple_gate_norm[i]) @ w_ple_gate[i])
        h = h + gate * rmsnorm(p[i] @ w_ple_proj[i], ple_proj_norm[i])

    return rmsnorm(h, final_norm)
```

```python
import functools

import numpy as np
import jax
import jax.numpy as jnp
from jax import lax
from jax.experimental import pallas as pl
from jax.experimental.pallas import tpu as pltpu

F32 = jnp.float32
BF16 = jnp.bfloat16

EPS = 1e-6
D_MODEL = 1024
D_FF = 2816
NH_M = 4
DV_M = 128
DK_M = 64
CONV_W = 4
NH_F = 8
DH_F = 64
D_MV = NH_M * DV_M
D_FV = NH_F * DH_F
N_GATES = 2 * NH_M + NH_F

LANES = 128
BF16_ROWS = 16
TM = 512
FF_CHUNK = 256
L_M = 256
T_F = 256
NEG = -1e30
VMEM_LIMIT = 56 * 1024 * 1024

NT = (((1,), (1,)), ((), ()))
TN = (((0,), (0,)), ((), ()))


def _rms(x, g):
    return x * lax.rsqrt(jnp.mean(x * x, axis=-1, keepdims=True) + EPS) * g


def _dot(a, b):
    return jnp.dot(a, b, preferred_element_type=F32)


def _split3(x):
    hi = x.astype(BF16)
    r = x - hi.astype(F32)
    mid = r.astype(BF16)
    lo = (r - mid.astype(F32)).astype(BF16)
    return hi, mid, lo


def _swiglu(x, norm_ref, wg_ref, wu_ref, wd_ref):
    xn = _rms(x, norm_ref[...]).astype(BF16)
    acc = jnp.zeros(x.shape, F32)
    for c in range(0, D_FF, FF_CHUNK):
        a = _dot(xn, wg_ref[:, c:c + FF_CHUNK])
        b = _dot(xn, wu_ref[:, c:c + FF_CHUNK])
        hmid = (a * jax.nn.sigmoid(a) * b).astype(BF16)
        acc = acc + _dot(hmid, wd_ref[c:c + FF_CHUNK, :])
    return acc


def _mix_in_kernel(x_ref, n1_ref, wg_ref, wu_ref, wd_ref, nm_ref, wnat_ref, wt_ref, conv_ref,
                   bias_ref, ucum_ref, ubd_ref, e48_ref,
                   h1_ref, mq_ref, mk_ref, mo_ref, mvt_ref, mg_ref, fq_ref, fk_ref, fvt_ref,
                   cbuf, carry):
    tm = x_ref.shape[0]

    @pl.when(pl.program_id(1) == 0)
    def _():
        cbuf[0:8, :] = jnp.zeros((8, cbuf.shape[1]), F32)
        carry[...] = jnp.zeros(carry.shape, F32)

    x = x_ref[...]
    h1 = x + 0.5 * _swiglu(x, n1_ref, wg_ref, wu_ref, wd_ref)
    h1_ref[...] = h1
    u = _rms(h1, nm_ref[...]).astype(BF16)

    lane = lax.broadcasted_iota(jnp.int32, (tm, LANES), 1)
    low = lane < DK_M

    def groups(z, lo_fill, hi_fill):
        cols = [z[:, LANES * j:LANES * (j + 1)] for j in range(z.shape[1] // LANES)]
        return ([jnp.where(low, c, lo_fill(j)) for j, c in enumerate(cols)]
                + [jnp.where(low, hi_fill(j), c) for j, c in enumerate(cols)])

    def store_groups(ref, gs):
        for g, v in enumerate(gs):
            ref[:, LANES * g:LANES * (g + 1)] = v.astype(ref.dtype)

    zqk = _dot(u, wnat_ref[:, 0:512])
    cbuf[8:8 + tm, :] = zqk
    cv = conv_ref[CONV_W - 1:CONV_W, :] * zqk
    for j in range(CONV_W - 1):
        cv = cv + conv_ref[j:j + 1, :] * cbuf[8 - (CONV_W - 1) + j:8 - (CONV_W - 1) + j + tm, :]
    cbuf[0:8, :] = cbuf[tm:tm + 8, :]
    s = cv * jax.nn.sigmoid(cv)
    zero = lambda j: 0.0
    store_groups(mq_ref, groups(s[:, 0:256] * (DK_M ** -0.5), zero, zero))
    store_groups(mk_ref, groups(s[:, 256:512], zero, zero))

    mo_ref[...] = _dot(u, wnat_ref[:, 512:1024]).astype(BF16)

    zt = lax.dot_general(wt_ref[...], u, NT, preferred_element_type=F32)
    mvt_ref[...] = zt[0:D_MV].astype(BF16)
    fvt_ref[...] = zt[D_MV:D_MV + D_FV].astype(BF16)
    g16 = zt[D_MV + D_FV:D_MV + D_FV + N_GATES] + jnp.tile(bias_ref[...], (1, tm // LANES))
    lf = jnp.minimum(g16, 0.0) - jnp.log1p(jnp.exp(-jnp.abs(g16)))
    xs = jnp.concatenate(_split3(lf), axis=0)
    def fold(r):
        return r[0:N_GATES] + r[N_GATES:2 * N_GATES] + r[2 * N_GATES:3 * N_GATES]
    cum = fold(_dot(xs, ucum_ref[...]))
    bch = fold(_dot(xs, ubd_ref[...]))
    tot = fold(_dot(xs, jnp.ones((tm, LANES), BF16)))
    c16 = cum + jnp.tile(carry[...], (1, tm // LANES))
    carry[...] = carry[...] + tot
    r8 = lax.broadcasted_iota(jnp.int32, (8, tm), 0)
    mg_ref[...] = jnp.where(r8 < NH_M, g16[0:8], bch[0:8])

    ys = jnp.concatenate(_split3(-c16), axis=0)
    place = lax.dot_general(ys, e48_ref[...], TN, preferred_element_type=F32)
    ones_lo = jnp.where((lane >= DH_F) & (lane < DH_F + 3), 1.0, 0.0)
    ones_hi = jnp.where(lane < 3, 1.0, 0.0)
    zfq = _dot(u, wnat_ref[:, 1024:1536]) * (DH_F ** -0.5)
    store_groups(fq_ref, groups(zfq, lambda j: ones_lo, lambda j: ones_hi))
    zfk = _dot(u, wnat_ref[:, 1536:2048])
    store_groups(fk_ref, groups(
        zfk,
        lambda j: place[:, LANES * j:LANES * (j + 1)],
        lambda j: place[:, LANES * (j + 4):LANES * (j + 5)]))


def _const_spec(shape):
    nd = len(shape)
    return pl.BlockSpec(shape, lambda *_: (0,) * nd, pipeline_mode=pl.Buffered(1))


def _mix_in(x, n1, wg, wu, wd, nm, wnat, wt, conv, bias16, ucum, ubd, e48):
    B, S, D = x.shape
    n_t = S // TM
    tok = lambda w: pl.BlockSpec((None, TM, w), lambda b, i: (b, i, 0))
    tokt = lambda r: pl.BlockSpec((None, r, TM), lambda b, i: (b, 0, i))
    consts = (n1, wg, wu, wd, nm, wnat, wt, conv, bias16, ucum, ubd, e48)
    out_shape = (
        jax.ShapeDtypeStruct((B, S, D), F32),
        jax.ShapeDtypeStruct((B, S, NH_M * LANES), BF16),
        jax.ShapeDtypeStruct((B, S, NH_M * LANES), BF16),
        jax.ShapeDtypeStruct((B, S, D_MV), BF16),
        jax.ShapeDtypeStruct((B, D_MV, S), BF16),
        jax.ShapeDtypeStruct((B, 8, S), F32),
        jax.ShapeDtypeStruct((B, S, NH_F * LANES), BF16),
        jax.ShapeDtypeStruct((B, S, NH_F * LANES), BF16),
        jax.ShapeDtypeStruct((B, D_FV, S), BF16),
    )
    out_specs = (tok(D), tok(NH_M * LANES), tok(NH_M * LANES), tok(D_MV), tokt(D_MV), tokt(8),
                 tok(NH_F * LANES), tok(NH_F * LANES), tokt(D_FV))
    return pl.pallas_call(
        _mix_in_kernel,
        grid=(B, n_t),
        in_specs=[tok(D)] + [_const_spec(c.shape) for c in consts],
        out_specs=out_specs,
        out_shape=out_shape,
        scratch_shapes=[pltpu.VMEM((TM + 8, 512), F32), pltpu.VMEM((N_GATES, LANES), F32)],
        compiler_params=pltpu.CompilerParams(
            dimension_semantics=("arbitrary", "arbitrary"), vmem_limit_bytes=VMEM_LIMIT),
        name="mix_in",
    )(x, *consts)


def _mlstm_kernel(q_ref, k_ref, vt_ref, mg_ref, mo_ref, gain_ref, o_ref, ct_ref):
    S = q_ref.shape[0]
    L = L_M
    ct_ref[...] = jnp.zeros(ct_ref.shape, F32)
    causal = (lax.broadcasted_iota(jnp.int32, (L, L), 0) <= lax.broadcasted_iota(jnp.int32, (L, L), 1))
    r16 = lax.broadcasted_iota(jnp.int32, (BF16_ROWS, L), 0)
    ones_rows = jnp.where(r16 < 3, 1.0, 0.0).astype(BF16)
    ones16 = jnp.ones((BF16_ROWS, L), BF16)

    def chunk(c, ms):
        cs = pl.multiple_of(c * L, L)
        g8 = mg_ref[:, pl.ds(cs, L)]
        new_ms = []
        for h in range(NH_M):
            hs = slice(LANES * h, LANES * (h + 1))
            m = ms[h]
            logi = g8[h:h + 1]
            b = g8[NH_M + h:NH_M + h + 1]
            a = logi - b
            gtot = b[:, L - 1:L]
            m_new = jnp.maximum(gtot + m, gtot + jnp.max(a, axis=1, keepdims=True))
            decay = jnp.exp(gtot + m - m_new)
            w = jnp.exp(a + gtot - m_new)

            ahi, amid, alo = (p.astype(F32) for p in _split3(a))
            at = jnp.where(r16 == 0, ahi, jnp.where(r16 == 1, amid, jnp.where(r16 == 2, alo, 0.0)))
            acol = lax.dot_general(at.astype(BF16), ones_rows, TN, preferred_element_type=F32)
            e = jnp.where(causal, acol, -jnp.inf)
            mu = jnp.maximum(m, jnp.max(e, axis=0, keepdims=True))

            q = q_ref[pl.ds(cs, L), hs]
            k = k_ref[pl.ds(cs, L), hs]
            st = lax.dot_general(k, q, NT, preferred_element_type=F32)
            pt = (jnp.exp(e - mu) * st).astype(BF16)
            vt_aug = jnp.concatenate([vt_ref[hs, pl.ds(cs, L)], ones16], axis=0)
            ct = ct_ref[h]
            nd = (jnp.exp(m - mu) * lax.dot_general(ct.astype(BF16), q, NT, preferred_element_type=F32)
                  + _dot(vt_aug, pt))
            den = nd[DV_M:DV_M + 1]
            rden = 1.0 / jnp.maximum(jnp.abs(den), jnp.exp(-(b + mu)))
            ht = nd[0:DV_M] * rden
            yt = (ht * lax.rsqrt(jnp.mean(ht * ht, axis=0, keepdims=True) + EPS)
                  * jnp.tile(gain_ref[hs, :], (1, L // LANES)))
            og = jax.nn.sigmoid(mo_ref[pl.ds(cs, L), hs].astype(F32))
            o_ref[pl.ds(cs, L), hs] = (yt.T * og).astype(o_ref.dtype)

            vtw = (vt_aug.astype(F32) * w).astype(BF16)
            ct_ref[h] = decay * ct + _dot(vtw, k)
            new_ms.append(m_new)
        return tuple(new_ms)

    lax.fori_loop(0, S // L, chunk, tuple(jnp.zeros((1, 1), F32) for _ in range(NH_M)))


def _mlstm(mq, mk, mvt, mg, mo, gain):
    B, S, _ = mq.shape
    tok = pl.BlockSpec((None, S, NH_M * LANES), lambda b: (b, 0, 0))
    return pl.pallas_call(
        _mlstm_kernel,
        grid=(B,),
        in_specs=[tok, tok,
                  pl.BlockSpec((None, D_MV, S), lambda b: (b, 0, 0)),
                  pl.BlockSpec((None, 8, S), lambda b: (b, 0, 0)),
                  pl.BlockSpec((None, S, D_MV), lambda b: (b, 0, 0)),
                  _const_spec(gain.shape)],
        out_specs=pl.BlockSpec((None, S, D_MV), lambda b: (b, 0, 0)),
        out_shape=jax.ShapeDtypeStruct((B, S, D_MV), BF16),
        scratch_shapes=[pltpu.VMEM((NH_M, DV_M + BF16_ROWS, LANES), F32)],
        compiler_params=pltpu.CompilerParams(
            dimension_semantics=("arbitrary",), vmem_limit_bytes=VMEM_LIMIT),
        name="mlstm",
    )(mq, mk, mvt, mg, mo, gain)


def _fox_kernel(q_ref, k_ref, vt_ref, gain_ref, o_ref):
    S = q_ref.shape[0]
    T = T_F
    causal = (lax.broadcasted_iota(jnp.int32, (T, T), 0) <= lax.broadcasted_iota(jnp.int32, (T, T), 1))
    ones16 = jnp.ones((BF16_ROWS, T), BF16)

    def qblock(qi, _):
        qs = pl.multiple_of(qi * T, T)
        qh = [q_ref[pl.ds(qs, T), LANES * j:LANES * (j + 1)] for j in range(2)]

        def step(ks, carry, masked):
            out = []
            for j in range(2):
                m, acc = carry[j]
                k = k_ref[pl.ds(ks, T), LANES * j:LANES * (j + 1)]
                st = lax.dot_general(k, qh[j], NT, preferred_element_type=F32)
                if masked:
                    st = jnp.where(causal, st, NEG)
                m_new = jnp.maximum(m, jnp.max(st, axis=0, keepdims=True))
                pt = jnp.exp(st - m_new).astype(BF16)
                vt_aug = jnp.concatenate([vt_ref[DH_F * j:DH_F * (j + 1), pl.ds(ks, T)], ones16], axis=0)
                acc = jnp.exp(m - m_new) * acc + _dot(vt_aug, pt)
                out.append((m_new, acc))
            return tuple(out)

        init = tuple((jnp.full((1, T), NEG, F32), jnp.zeros((DH_F + BF16_ROWS, T), F32)) for _ in range(2))
        carry = lax.fori_loop(0, qi, lambda kj, c: step(pl.multiple_of(kj * T, T), c, False), init)
        carry = step(qs, carry, True)
        ys = []
        for j in range(2):
            _, acc = carry[j]
            o = acc[0:DH_F] * (1.0 / acc[DH_F:DH_F + 1])
            ys.append(o * lax.rsqrt(jnp.mean(o * o, axis=0, keepdims=True) + EPS)
                      * jnp.tile(gain_ref[DH_F * j:DH_F * (j + 1), :], (1, T // LANES)))
        o_ref[pl.ds(qs, T), :] = jnp.concatenate(ys, axis=0).T.astype(o_ref.dtype)
        return 0

    lax.fori_loop(0, S // T, qblock, 0)


def _fox(fq, fk, fvt, gain):
    B, S, _ = fq.shape
    n_pair = NH_F // 2
    tok = pl.BlockSpec((None, S, 2 * LANES), lambda b, p: (b, 0, p))
    return pl.pallas_call(
        _fox_kernel,
        grid=(B, n_pair),
        in_specs=[tok, tok,
                  pl.BlockSpec((None, 2 * DH_F, S), lambda b, p: (b, p, 0)),
                  pl.BlockSpec((2 * DH_F, LANES), lambda b, p: (p, 0))],
        out_specs=pl.BlockSpec((None, S, 2 * DH_F), lambda b, p: (b, 0, p)),
        out_shape=jax.ShapeDtypeStruct((B, S, D_FV), BF16),
        compiler_params=pltpu.CompilerParams(
            dimension_semantics=("arbitrary", "arbitrary"), vmem_limit_bytes=VMEM_LIMIT),
        name="fox",
    )(fq, fk, fvt, gain)


def _mix_out_kernel(h_ref, ym_ref, yf_ref, p_ref, wom_ref, wof_ref, n2_ref, wg_ref, wu_ref, wd_ref,
                    ng_ref, wpg_ref, wpp_ref, npp_ref, nf_ref, o_ref):
    h = h_ref[...] + _dot(ym_ref[...], wom_ref[...]) + _dot(yf_ref[...], wof_ref[...])
    h = h + 0.5 * _swiglu(h, n2_ref, wg_ref, wu_ref, wd_ref)
    gate = jax.nn.sigmoid(_dot(_rms(h, ng_ref[...]).astype(BF16), wpg_ref[...]))
    pe = _rms(_dot(p_ref[...].astype(BF16), wpp_ref[...]), npp_ref[...])
    h = h + gate * pe
    o_ref[...] = _rms(h, nf_ref[...])


def _mix_out(h1, ym, yf, p, *consts):
    B, S, D = h1.shape
    tok = lambda w: pl.BlockSpec((None, TM, w), lambda b, i: (b, i, 0))
    return pl.pallas_call(
        _mix_out_kernel,
        grid=(B, S // TM),
        in_specs=[tok(D), tok(D_MV), tok(D_FV), tok(p.shape[-1])] + [_const_spec(c.shape) for c in consts],
        out_specs=tok(D),
        out_shape=jax.ShapeDtypeStruct((B, S, D), F32),
        compiler_params=pltpu.CompilerParams(
            dimension_semantics=("arbitrary", "arbitrary"), vmem_limit_bytes=VMEM_LIMIT),
        name="mix_out",
    )(h1, ym, yf, p, *consts)


def _interleave(base, n_heads, width):
    half = n_heads // 2
    idx = []
    for j in range(half):
        idx += list(range(base + j * width, base + (j + 1) * width))
        idx += list(range(base + (j + half) * width, base + (j + half + 1) * width))
    return idx


_O_MQ, _O_MK, _O_MV, _O_MO, _O_MIF = 0, 256, 512, 1024, 1536
_O_FQ, _O_FK, _O_FV, _O_FF = 1544, 2056, 2568, 3080
_QK_PERM = np.array(_interleave(0, NH_M, DK_M) + _interleave(256, NH_M, DK_M))
_NAT_COLS = np.concatenate([
    _QK_PERM, np.arange(_O_MO, _O_MO + D_MV),
    np.array(_interleave(_O_FQ, NH_F, DH_F)), np.array(_interleave(_O_FK, NH_F, DH_F))])
_T_COLS = np.concatenate([np.arange(_O_MV, _O_MV + D_MV), np.arange(_O_FV, _O_FV + D_FV),
                          np.arange(_O_MIF, _O_MIF + 2 * NH_M), np.arange(_O_FF, _O_FF + NH_F)])


def _placement():
    e = np.zeros((3 * N_GATES, NH_F * LANES), np.float32)
    for j in range(3):
        for h in range(NH_F):
            e[N_GATES * j + 2 * NH_M + h, LANES * h + (DH_F + j if h < NH_F // 2 else j)] = 1.0
    return e


def _tri(n, block):
    s = np.arange(n)[:, None]
    t = np.arange(n)[None, :]
    return ((s <= t) & (s // block == t // block)).astype(np.float32)


def kernel(x, p, ffn1_norm, ffn1_w_gate, ffn1_w_up, ffn1_w_down, mix_norm, w_in, conv_qk, b_mlstm_gates, b_fox_f, mlstm_out_norm, fox_out_norm, w_out, ffn2_norm, ffn2_w_gate, ffn2_w_up, ffn2_w_down, ple_gate_norm, w_ple_gate, w_ple_proj, ple_proj_norm, final_norm):
    depth = w_in.shape[0]
    assert depth == 1, "the final RMSNorm is fused into the (single) layer's last kernel"
    assert x.shape[1] % TM == 0 and x.shape[2] == D_MODEL
    row = lambda v: v.reshape(1, -1).astype(F32)
    col = lambda v: jnp.broadcast_to(v.astype(F32)[:, None], (v.shape[0], LANES))
    ucum = jnp.asarray(_tri(TM, TM), BF16)
    ubd = jnp.asarray(_tri(TM, L_M), BF16)
    e48 = jnp.asarray(_placement(), BF16)
    h = x
    for i in range(depth):
        wnat = w_in[i][:, _NAT_COLS].astype(BF16)
        wt = w_in[i][:, _T_COLS].T.astype(BF16)
        conv = conv_qk[i][:, _QK_PERM].astype(F32)
        bias16 = col(jnp.concatenate([b_mlstm_gates[i], b_fox_f[i]]))
        h1, mq, mk, mo, mvt, mg, fq, fk, fvt = _mix_in(
            h, row(ffn1_norm[i]), ffn1_w_gate[i].astype(BF16), ffn1_w_up[i].astype(BF16),
            ffn1_w_down[i].astype(BF16), row(mix_norm[i]), wnat, wt, conv, bias16, ucum, ubd, e48)
        ym = _mlstm(mq, mk, mvt, mg, mo, col(mlstm_out_norm[i]))
        yf = _fox(fq, fk, fvt, col(fox_out_norm[i]))
        wo = w_out[i].astype(BF16)
        h = _mix_out(
            h1, ym, yf, p[i], wo[:D_MV], wo[D_MV:], row(ffn2_norm[i]), ffn2_w_gate[i].astype(BF16),
            ffn2_w_up[i].astype(BF16), ffn2_w_down[i].astype(BF16), row(ple_gate_norm[i]),
            w_ple_gate[i].astype(BF16), w_ple_proj[i].astype(BF16), row(ple_proj_norm[i]),
            row(final_norm))
    return h
```

```python
import functools

import numpy as np
import jax
import jax.numpy as jnp
from jax import lax
from jax.experimental import pallas as pl
from jax.experimental.pallas import tpu as pltpu

F32 = jnp.float32
BF16 = jnp.bfloat16

EPS = 1e-6
D_MODEL = 1024
D_FF = 2816
NH_M = 4
DV_M = 128
DK_M = 64
CONV_W = 4
NH_F = 8
DH_F = 64
D_MV = NH_M * DV_M
D_FV = NH_F * DH_F
N_GATES = 2 * NH_M + NH_F

LANES = 128
BF16_ROWS = 16
TM = 512
FF_CHUNK = 256
L_M = 256
T_F = 512
LOG2E = 1.4426950408889634
NEG = -1e30
VMEM_LIMIT = 56 * 1024 * 1024

NT = (((1,), (1,)), ((), ()))
TN = (((0,), (0,)), ((), ()))


def _rms(x, g):
    return x * lax.rsqrt(jnp.mean(x * x, axis=-1, keepdims=True) + EPS) * g


def _dot(a, b):
    return jnp.dot(a, b, preferred_element_type=F32)


def _split3(x):
    hi = x.astype(BF16)
    r = x - hi.astype(F32)
    mid = r.astype(BF16)
    lo = (r - mid.astype(F32)).astype(BF16)
    return hi, mid, lo


def _swiglu(x, norm_ref, wg_ref, wu_ref, wd_ref):
    xn = _rms(x, norm_ref[...]).astype(BF16)
    acc = jnp.zeros(x.shape, F32)
    for c in range(0, D_FF, FF_CHUNK):
        a = _dot(xn, wg_ref[:, c:c + FF_CHUNK])
        b = _dot(xn, wu_ref[:, c:c + FF_CHUNK])
        hmid = (a * jax.nn.sigmoid(a) * b).astype(BF16)
        acc = acc + _dot(hmid, wd_ref[c:c + FF_CHUNK, :])
    return acc


def _mix_in_kernel(x_ref, n1_ref, wg_ref, wu_ref, wd_ref, nm_ref, wnat_ref, wt_ref, conv_ref,
                   bias_ref, ucum_ref, ubd_ref, e48_ref,
                   h1_ref, mq_ref, mk_ref, mo_ref, mvt_ref, mg_ref, fq_ref, fk_ref, fvt_ref,
                   cbuf, carry):
    tm = x_ref.shape[0]

    @pl.when(pl.program_id(1) == 0)
    def _():
        cbuf[0:8, :] = jnp.zeros((8, cbuf.shape[1]), F32)
        carry[...] = jnp.zeros(carry.shape, F32)

    x = x_ref[...]
    h1 = x + 0.5 * _swiglu(x, n1_ref, wg_ref, wu_ref, wd_ref)
    h1_ref[...] = h1
    u = _rms(h1, nm_ref[...]).astype(BF16)

    lane = lax.broadcasted_iota(jnp.int32, (tm, LANES), 1)
    low = lane < DK_M

    def groups(z, lo_fill, hi_fill):
        cols = [z[:, LANES * j:LANES * (j + 1)] for j in range(z.shape[1] // LANES)]
        return ([jnp.where(low, c, lo_fill(j)) for j, c in enumerate(cols)]
                + [jnp.where(low, hi_fill(j), c) for j, c in enumerate(cols)])

    def store_groups(ref, gs):
        for g, v in enumerate(gs):
            ref[:, LANES * g:LANES * (g + 1)] = v.astype(ref.dtype)

    zqk = _dot(u, wnat_ref[:, 0:512])
    cbuf[8:8 + tm, :] = zqk
    cv = conv_ref[CONV_W - 1:CONV_W, :] * zqk
    for j in range(CONV_W - 1):
        cv = cv + conv_ref[j:j + 1, :] * cbuf[8 - (CONV_W - 1) + j:8 - (CONV_W - 1) + j + tm, :]
    cbuf[0:8, :] = cbuf[tm:tm + 8, :]
    s = cv * jax.nn.sigmoid(cv)
    zero = lambda j: 0.0
    store_groups(mq_ref, groups(s[:, 0:256] * (DK_M ** -0.5), zero, zero))
    store_groups(mk_ref, groups(s[:, 256:512], zero, zero))

    mo_ref[...] = _dot(u, wnat_ref[:, 512:1024]).astype(BF16)

    zt = lax.dot_general(wt_ref[...], u, NT, preferred_element_type=F32)
    mvt_ref[...] = zt[0:D_MV].astype(BF16)
    fvt_ref[...] = zt[D_MV:D_MV + D_FV].astype(BF16)
    g16 = zt[D_MV + D_FV:D_MV + D_FV + N_GATES] + jnp.tile(bias_ref[...], (1, tm // LANES))
    lf = jnp.minimum(g16, 0.0) - jnp.log1p(jnp.exp(-jnp.abs(g16)))
    xs = jnp.concatenate(_split3(lf), axis=0)
    def fold(r):
        return r[0:N_GATES] + r[N_GATES:2 * N_GATES] + r[2 * N_GATES:3 * N_GATES]
    cum = fold(_dot(xs, ucum_ref[...]))
    bch = fold(_dot(xs, ubd_ref[...]))
    tot = fold(_dot(xs, jnp.ones((tm, LANES), BF16)))
    c16 = cum + jnp.tile(carry[...], (1, tm // LANES))
    carry[...] = carry[...] + tot
    r8 = lax.broadcasted_iota(jnp.int32, (8, tm), 0)
    mg_ref[...] = jnp.where(r8 < NH_M, g16[0:8], bch[0:8])

    ys = jnp.concatenate(_split3(-LOG2E * c16), axis=0)
    place = lax.dot_general(ys, e48_ref[...], TN, preferred_element_type=F32)
    ones_lo = jnp.where((lane >= DH_F) & (lane < DH_F + 3), 1.0, 0.0)
    ones_hi = jnp.where(lane < 3, 1.0, 0.0)
    zfq = _dot(u, wnat_ref[:, 1024:1536]) * (DH_F ** -0.5 * LOG2E)
    store_groups(fq_ref, groups(zfq, lambda j: ones_lo, lambda j: ones_hi))
    zfk = _dot(u, wnat_ref[:, 1536:2048])
    store_groups(fk_ref, groups(
        zfk,
        lambda j: place[:, LANES * j:LANES * (j + 1)],
        lambda j: place[:, LANES * (j + 4):LANES * (j + 5)]))


def _const_spec(shape):
    nd = len(shape)
    return pl.BlockSpec(shape, lambda *_: (0,) * nd, pipeline_mode=pl.Buffered(1))


def _mix_in(x, n1, wg, wu, wd, nm, wnat, wt, conv, bias16, ucum, ubd, e48):
    B, S, D = x.shape
    n_t = S // TM
    tok = lambda w: pl.BlockSpec((None, TM, w), lambda b, i: (b, i, 0))
    tokt = lambda r: pl.BlockSpec((None, r, TM), lambda b, i: (b, 0, i))
    consts = (n1, wg, wu, wd, nm, wnat, wt, conv, bias16, ucum, ubd, e48)
    out_shape = (
        jax.ShapeDtypeStruct((B, S, D), F32),
        jax.ShapeDtypeStruct((B, S, NH_M * LANES), BF16),
        jax.ShapeDtypeStruct((B, S, NH_M * LANES), BF16),
        jax.ShapeDtypeStruct((B, S, D_MV), BF16),
        jax.ShapeDtypeStruct((B, D_MV, S), BF16),
        jax.ShapeDtypeStruct((B, 8, S), F32),
        jax.ShapeDtypeStruct((B, S, NH_F * LANES), BF16),
        jax.ShapeDtypeStruct((B, S, NH_F * LANES), BF16),
        jax.ShapeDtypeStruct((B, D_FV, S), BF16),
    )
    out_specs = (tok(D), tok(NH_M * LANES), tok(NH_M * LANES), tok(D_MV), tokt(D_MV), tokt(8),
                 tok(NH_F * LANES), tok(NH_F * LANES), tokt(D_FV))
    return pl.pallas_call(
        _mix_in_kernel,
        grid=(B, n_t),
        in_specs=[tok(D)] + [_const_spec(c.shape) for c in consts],
        out_specs=out_specs,
        out_shape=out_shape,
        scratch_shapes=[pltpu.VMEM((TM + 8, 512), F32), pltpu.VMEM((N_GATES, LANES), F32)],
        compiler_params=pltpu.CompilerParams(
            dimension_semantics=("arbitrary", "arbitrary"), vmem_limit_bytes=VMEM_LIMIT),
        name="mix_in",
    )(x, *consts)


def _mlstm_kernel(q_ref, k_ref, vt_ref, mg_ref, mo_ref, gain_ref, o_ref, ct_ref):
    S = q_ref.shape[0]
    L = L_M
    ct_ref[...] = jnp.zeros(ct_ref.shape, F32)
    causal = (lax.broadcasted_iota(jnp.int32, (L, L), 0) <= lax.broadcasted_iota(jnp.int32, (L, L), 1))
    r16 = lax.broadcasted_iota(jnp.int32, (BF16_ROWS, L), 0)
    ones_rows = jnp.where(r16 < 3, 1.0, 0.0).astype(BF16)
    ones16 = jnp.ones((BF16_ROWS, L), BF16)

    def chunk(c, ms):
        cs = pl.multiple_of(c * L, L)
        g8 = mg_ref[:, pl.ds(cs, L)]
        new_ms = []
        for h in range(NH_M):
            hs = slice(LANES * h, LANES * (h + 1))
            m = ms[h]
            logi = g8[h:h + 1]
            b = g8[NH_M + h:NH_M + h + 1]
            a = logi - b
            gtot = b[:, L - 1:L]
            m_new = jnp.maximum(gtot + m, gtot + jnp.max(a, axis=1, keepdims=True))
            decay = jnp.exp(gtot + m - m_new)
            w = jnp.exp(a + gtot - m_new)

            ahi, amid, alo = (p.astype(F32) for p in _split3(a))
            at = jnp.where(r16 == 0, ahi, jnp.where(r16 == 1, amid, jnp.where(r16 == 2, alo, 0.0)))
            acol = lax.dot_general(at.astype(BF16), ones_rows, TN, preferred_element_type=F32)
            e = jnp.where(causal, acol, -jnp.inf)
            mu = jnp.maximum(m, jnp.max(e, axis=0, keepdims=True))

            q = q_ref[pl.ds(cs, L), hs]
            k = k_ref[pl.ds(cs, L), hs]
            st = lax.dot_general(k, q, NT, preferred_element_type=F32)
            pt = (jnp.exp(e - mu) * st).astype(BF16)
            vt_aug = jnp.concatenate([vt_ref[hs, pl.ds(cs, L)], ones16], axis=0)
            ct = ct_ref[h]
            nd = (jnp.exp(m - mu) * lax.dot_general(ct.astype(BF16), q, NT, preferred_element_type=F32)
                  + _dot(vt_aug, pt))
            den = nd[DV_M:DV_M + 1]
            rden = 1.0 / jnp.maximum(jnp.abs(den), jnp.exp(-(b + mu)))
            ht = nd[0:DV_M] * rden
            yt = (ht * lax.rsqrt(jnp.mean(ht * ht, axis=0, keepdims=True) + EPS)
                  * jnp.tile(gain_ref[hs, :], (1, L // LANES)))
            og = jax.nn.sigmoid(mo_ref[pl.ds(cs, L), hs].astype(F32))
            o_ref[pl.ds(cs, L), hs] = (yt.T * og).astype(o_ref.dtype)

            vtw = (vt_aug.astype(F32) * w).astype(BF16)
            ct_ref[h] = decay * ct + _dot(vtw, k)
            new_ms.append(m_new)
        return tuple(new_ms)

    lax.fori_loop(0, S // L, chunk, tuple(jnp.zeros((1, 1), F32) for _ in range(NH_M)))


def _mlstm(mq, mk, mvt, mg, mo, gain):
    B, S, _ = mq.shape
    tok = pl.BlockSpec((None, S, NH_M * LANES), lambda b: (b, 0, 0))
    return pl.pallas_call(
        _mlstm_kernel,
        grid=(B,),
        in_specs=[tok, tok,
                  pl.BlockSpec((None, D_MV, S), lambda b: (b, 0, 0)),
                  pl.BlockSpec((None, 8, S), lambda b: (b, 0, 0)),
                  pl.BlockSpec((None, S, D_MV), lambda b: (b, 0, 0)),
                  _const_spec(gain.shape)],
        out_specs=pl.BlockSpec((None, S, D_MV), lambda b: (b, 0, 0)),
        out_shape=jax.ShapeDtypeStruct((B, S, D_MV), BF16),
        scratch_shapes=[pltpu.VMEM((NH_M, DV_M + BF16_ROWS, LANES), F32)],
        compiler_params=pltpu.CompilerParams(
            dimension_semantics=("arbitrary",), vmem_limit_bytes=VMEM_LIMIT),
        name="mlstm",
    )(mq, mk, mvt, mg, mo, gain)


def _fox_kernel(q_ref, k_ref, vt_ref, gain_ref, o_ref, st_a, st_b, mb_a, mb_b, m_ref, acc_ref, bias_ref):
    S = q_ref.shape[0]
    T = T_F
    n_q = S // T
    heads = [slice(LANES * j, LANES * (j + 1)) for j in range(2)]
    bufs = ((st_a, mb_a), (st_b, mb_b))
    ones16 = jnp.ones((BF16_ROWS, T), BF16)
    visible = lax.broadcasted_iota(jnp.int32, (T, T), 0) <= lax.broadcasted_iota(jnp.int32, (T, T), 1)
    bias_ref[...] = jnp.where(visible, 0.0, NEG)

    def logits(c, buf, diag):
        qi, kj = c
        st_buf, mb_buf = buf
        qs = pl.multiple_of(qi * T, T)
        ks = pl.multiple_of(kj * T, T)
        for j, hs in enumerate(heads):
            st = lax.dot_general(k_ref[pl.ds(ks, T), hs], q_ref[pl.ds(qs, T), hs], NT, preferred_element_type=F32)
            if diag:
                st = st + bias_ref[...]
            st_buf[j] = st
            mb_buf[j] = jnp.max(st, axis=0, keepdims=True)

    def update(c, buf, first):
        qi, kj = c
        st_buf, mb_buf = buf
        ks = pl.multiple_of(kj * T, T)
        for j in range(2):
            m_new = mb_buf[j] if first else jnp.maximum(m_ref[qi, j], mb_buf[j])
            pt = jnp.exp2(st_buf[j] - m_new).astype(BF16)
            vt_aug = jnp.concatenate([vt_ref[DH_F * j:DH_F * (j + 1), pl.ds(ks, T)], ones16], axis=0)
            pv = _dot(vt_aug, pt)
            acc_ref[qi, j] = pv if first else jnp.exp2(m_ref[qi, j] - m_new) * acc_ref[qi, j] + pv
            m_ref[qi, j] = m_new

    def sweep(n, start, succ, diag):
        if n == 0:
            return
        logits(start, bufs[0], diag)

        def clamp(c, prev):
            over = c[0] >= n_q
            return jnp.where(over, prev[0], c[0]), jnp.where(over, prev[1], c[1])

        def two_steps(_, c0):
            c1 = succ(c0)
            logits(c1, bufs[1], diag)
            update(c0, bufs[0], diag)
            c2 = clamp(succ(c1), c1)
            logits(c2, bufs[0], diag)
            update(c1, bufs[1], diag)
            return c2

        last = lax.fori_loop(0, n // 2, two_steps, start)
        if n % 2:
            update(last, bufs[0], diag)

    zero = jnp.int32(0)
    sweep(n_q, (zero, zero), lambda c: (c[0] + 1, c[1] + 1), True)

    def below(c):
        wrap = c[1] + 1 >= c[0]
        return jnp.where(wrap, c[0] + 1, c[0]), jnp.where(wrap, 0, c[1] + 1)

    sweep(n_q * (n_q - 1) // 2, (zero + 1, zero), below, False)

    def finish(qi, _):
        ys = []
        for j in range(2):
            acc = acc_ref[qi, j]
            o = acc[0:DH_F] * (1.0 / acc[DH_F:DH_F + 1])
            ys.append(o * lax.rsqrt(jnp.mean(o * o, axis=0, keepdims=True) + EPS)
                      * jnp.tile(gain_ref[DH_F * j:DH_F * (j + 1), :], (1, T // LANES)))
        o_ref[pl.ds(pl.multiple_of(qi * T, T), T), :] = jnp.concatenate(ys, axis=0).T.astype(o_ref.dtype)
        return 0

    lax.fori_loop(0, n_q, finish, 0)


def _fox(fq, fk, fvt, gain):
    B, S, _ = fq.shape
    n_pair = NH_F // 2
    tok = pl.BlockSpec((None, S, 2 * LANES), lambda b, p: (b, 0, p))
    return pl.pallas_call(
        _fox_kernel,
        grid=(B, n_pair),
        in_specs=[tok, tok,
                  pl.BlockSpec((None, 2 * DH_F, S), lambda b, p: (b, p, 0)),
                  pl.BlockSpec((2 * DH_F, LANES), lambda b, p: (p, 0))],
        out_specs=pl.BlockSpec((None, S, 2 * DH_F), lambda b, p: (b, 0, p)),
        out_shape=jax.ShapeDtypeStruct((B, S, D_FV), BF16),
        scratch_shapes=[pltpu.VMEM((2, T_F, T_F), F32), pltpu.VMEM((2, T_F, T_F), F32),
                        pltpu.VMEM((2, 1, T_F), F32), pltpu.VMEM((2, 1, T_F), F32),
                        pltpu.VMEM((S // T_F, 2, 1, T_F), F32),
                        pltpu.VMEM((S // T_F, 2, DH_F + BF16_ROWS, T_F), F32),
                        pltpu.VMEM((T_F, T_F), F32)],
        compiler_params=pltpu.CompilerParams(
            dimension_semantics=("arbitrary", "arbitrary"), vmem_limit_bytes=VMEM_LIMIT),
        name="fox",
    )(fq, fk, fvt, gain)


def _mix_out_kernel(h_ref, ym_ref, yf_ref, p_ref, wom_ref, wof_ref, n2_ref, wg_ref, wu_ref, wd_ref,
                    ng_ref, wpg_ref, wpp_ref, npp_ref, nf_ref, o_ref):
    h = h_ref[...] + _dot(ym_ref[...], wom_ref[...]) + _dot(yf_ref[...], wof_ref[...])
    h = h + 0.5 * _swiglu(h, n2_ref, wg_ref, wu_ref, wd_ref)
    gate = jax.nn.sigmoid(_dot(_rms(h, ng_ref[...]).astype(BF16), wpg_ref[...]))
    pe = _rms(_dot(p_ref[...].astype(BF16), wpp_ref[...]), npp_ref[...])
    h = h + gate * pe
    o_ref[...] = _rms(h, nf_ref[...])


def _mix_out(h1, ym, yf, p, *consts):
    B, S, D = h1.shape
    tok = lambda w: pl.BlockSpec((None, TM, w), lambda b, i: (b, i, 0))
    return pl.pallas_call(
        _mix_out_kernel,
        grid=(B, S // TM),
        in_specs=[tok(D), tok(D_MV), tok(D_FV), tok(p.shape[-1])] + [_const_spec(c.shape) for c in consts],
        out_specs=tok(D),
        out_shape=jax.ShapeDtypeStruct((B, S, D), F32),
        compiler_params=pltpu.CompilerParams(
            dimension_semantics=("arbitrary", "arbitrary"), vmem_limit_bytes=VMEM_LIMIT),
        name="mix_out",
    )(h1, ym, yf, p, *consts)


def _interleave(base, n_heads, width):
    half = n_heads // 2
    idx = []
    for j in range(half):
        idx += list(range(base + j * width, base + (j + 1) * width))
        idx += list(range(base + (j + half) * width, base + (j + half + 1) * width))
    return idx


_O_MQ, _O_MK, _O_MV, _O_MO, _O_MIF = 0, 256, 512, 1024, 1536
_O_FQ, _O_FK, _O_FV, _O_FF = 1544, 2056, 2568, 3080
_QK_PERM = np.array(_interleave(0, NH_M, DK_M) + _interleave(256, NH_M, DK_M))
_NAT_COLS = np.concatenate([
    _QK_PERM, np.arange(_O_MO, _O_MO + D_MV),
    np.array(_interleave(_O_FQ, NH_F, DH_F)), np.array(_interleave(_O_FK, NH_F, DH_F))])
_T_COLS = np.concatenate([np.arange(_O_MV, _O_MV + D_MV), np.arange(_O_FV, _O_FV + D_FV),
                          np.arange(_O_MIF, _O_MIF + 2 * NH_M), np.arange(_O_FF, _O_FF + NH_F)])


def _placement():
    e = np.zeros((3 * N_GATES, NH_F * LANES), np.float32)
    for j in range(3):
        for h in range(NH_F):
            e[N_GATES * j + 2 * NH_M + h, LANES * h + (DH_F + j if h < NH_F // 2 else j)] = 1.0
    return e


def _tri(n, block):
    s = np.arange(n)[:, None]
    t = np.arange(n)[None, :]
    return ((s <= t) & (s // block == t // block)).astype(np.float32)


def kernel(x, p, ffn1_norm, ffn1_w_gate, ffn1_w_up, ffn1_w_down, mix_norm, w_in, conv_qk, b_mlstm_gates, b_fox_f, mlstm_out_norm, fox_out_norm, w_out, ffn2_norm, ffn2_w_gate, ffn2_w_up, ffn2_w_down, ple_gate_norm, w_ple_gate, w_ple_proj, ple_proj_norm, final_norm):
    depth = w_in.shape[0]
    assert depth == 1, "the final RMSNorm is fused into the (single) layer's last kernel"
    assert x.shape[1] % TM == 0 and x.shape[2] == D_MODEL
    row = lambda v: v.reshape(1, -1).astype(F32)
    col = lambda v: jnp.broadcast_to(v.astype(F32)[:, None], (v.shape[0], LANES))
    ucum = jnp.asarray(_tri(TM, TM), BF16)
    ubd = jnp.asarray(_tri(TM, L_M), BF16)
    e48 = jnp.asarray(_placement(), BF16)
    h = x
    for i in range(depth):
        wnat = w_in[i][:, _NAT_COLS].astype(BF16)
        wt = w_in[i][:, _T_COLS].T.astype(BF16)
        conv = conv_qk[i][:, _QK_PERM].astype(F32)
        bias16 = col(jnp.concatenate([b_mlstm_gates[i], b_fox_f[i]]))
        h1, mq, mk, mo, mvt, mg, fq, fk, fvt = _mix_in(
            h, row(ffn1_norm[i]), ffn1_w_gate[i].astype(BF16), ffn1_w_up[i].astype(BF16),
            ffn1_w_down[i].astype(BF16), row(mix_norm[i]), wnat, wt, conv, bias16, ucum, ubd, e48)
        ym = _mlstm(mq, mk, mvt, mg, mo, col(mlstm_out_norm[i]))
        yf = _fox(fq, fk, fvt, col(fox_out_norm[i]))
        wo = w_out[i].astype(BF16)
        h = _mix_out(
            h1, ym, yf, p[i], wo[:D_MV], wo[D_MV:], row(ffn2_norm[i]), ffn2_w_gate[i].astype(BF16),
            ffn2_w_up[i].astype(BF16), ffn2_w_down[i].astype(BF16), row(ple_gate_norm[i]),
            w_ple_gate[i].astype(BF16), w_ple_proj[i].astype(BF16), row(ple_proj_norm[i]),
            row(final_norm))
    return h
```

```python
import functools

import numpy as np
import jax
import jax.numpy as jnp
from jax import lax
from jax.experimental import pallas as pl
from jax.experimental.pallas import tpu as pltpu

F32 = jnp.float32
BF16 = jnp.bfloat16

EPS = 1e-6
D_MODEL = 1024
D_FF = 2816
NH_M = 4
DV_M = 128
DK_M = 64
CONV_W = 4
NH_F = 8
DH_F = 64
D_MV = NH_M * DV_M
D_FV = NH_F * DH_F
N_GATES = 2 * NH_M + NH_F

LANES = 128
BF16_ROWS = 16
MXU_COLS = 256
TM = 512
FF_CHUNK = 256
L_M = 256
T_F = 512
LOG2E = 1.4426950408889634
FOX_UNROLL = 4
NEG = -1e30
VMEM_LIMIT = 56 * 1024 * 1024

NT = (((1,), (1,)), ((), ()))
TN = (((0,), (0,)), ((), ()))


def _rms(x, g):
    return x * lax.rsqrt(jnp.mean(x * x, axis=-1, keepdims=True) + EPS) * g


def _dot(a, b):
    return jnp.dot(a, b, preferred_element_type=F32)


def _split3(x):
    hi = x.astype(BF16)
    r = x - hi.astype(F32)
    mid = r.astype(BF16)
    lo = (r - mid.astype(F32)).astype(BF16)
    return hi, mid, lo


def _swiglu(x, norm_ref, wg_ref, wu_ref, wd_ref):
    xn = _rms(x, norm_ref[...]).astype(BF16)
    acc = jnp.zeros(x.shape, F32)
    for c in range(0, D_FF, FF_CHUNK):
        a = _dot(xn, wg_ref[:, c:c + FF_CHUNK])
        b = _dot(xn, wu_ref[:, c:c + FF_CHUNK])
        hmid = (a * jax.nn.sigmoid(a) * b).astype(BF16)
        acc = acc + _dot(hmid, wd_ref[c:c + FF_CHUNK, :])
    return acc


def _mix_in_kernel(x_ref, n1_ref, wg_ref, wu_ref, wd_ref, nm_ref, wnat_ref, wt_ref, conv_ref,
                   bias_ref, ucum_ref, ubd_ref, e48_ref,
                   h1_ref, mq_ref, mk_ref, mo_ref, mvt_ref, mg_ref, fq_ref, fk_ref, fvt_ref,
                   cbuf, carry):
    tm = x_ref.shape[0]

    @pl.when(pl.program_id(1) == 0)
    def _():
        cbuf[0:8, :] = jnp.zeros((8, cbuf.shape[1]), F32)
        carry[...] = jnp.zeros(carry.shape, F32)

    x = x_ref[...]
    h1 = x + 0.5 * _swiglu(x, n1_ref, wg_ref, wu_ref, wd_ref)
    h1_ref[...] = h1
    u = _rms(h1, nm_ref[...]).astype(BF16)

    lane = lax.broadcasted_iota(jnp.int32, (tm, LANES), 1)
    low = lane < DK_M

    def groups(z, lo_fill, hi_fill):
        cols = [z[:, LANES * j:LANES * (j + 1)] for j in range(z.shape[1] // LANES)]
        return ([jnp.where(low, c, lo_fill(j)) for j, c in enumerate(cols)]
                + [jnp.where(low, hi_fill(j), c) for j, c in enumerate(cols)])

    def store_groups(ref, gs):
        for g, v in enumerate(gs):
            ref[:, LANES * g:LANES * (g + 1)] = v.astype(ref.dtype)

    zqk = _dot(u, wnat_ref[:, 0:512])
    cbuf[8:8 + tm, :] = zqk
    cv = conv_ref[CONV_W - 1:CONV_W, :] * zqk
    for j in range(CONV_W - 1):
        cv = cv + conv_ref[j:j + 1, :] * cbuf[8 - (CONV_W - 1) + j:8 - (CONV_W - 1) + j + tm, :]
    cbuf[0:8, :] = cbuf[tm:tm + 8, :]
    s = cv * jax.nn.sigmoid(cv)
    zero = lambda j: 0.0
    store_groups(mq_ref, groups(s[:, 0:256] * (DK_M ** -0.5), zero, zero))
    store_groups(mk_ref, groups(s[:, 256:512], zero, zero))

    mo_ref[...] = _dot(u, wnat_ref[:, 512:1024]).astype(BF16)

    zt = lax.dot_general(wt_ref[...], u, NT, preferred_element_type=F32)
    mvt_ref[...] = zt[0:D_MV].astype(BF16)
    fvt_ref[...] = zt[D_MV:D_MV + D_FV].astype(BF16)
    g16 = zt[D_MV + D_FV:D_MV + D_FV + N_GATES] + jnp.tile(bias_ref[...], (1, tm // LANES))
    lf = jnp.minimum(g16, 0.0) - jnp.log1p(jnp.exp(-jnp.abs(g16)))
    xs = jnp.concatenate(_split3(lf), axis=0)
    def fold(r):
        return r[0:N_GATES] + r[N_GATES:2 * N_GATES] + r[2 * N_GATES:3 * N_GATES]
    cum = fold(_dot(xs, ucum_ref[...]))
    bch = fold(_dot(xs, ubd_ref[...]))
    tot = fold(_dot(xs, jnp.ones((tm, LANES), BF16)))
    c16 = cum + jnp.tile(carry[...], (1, tm // LANES))
    carry[...] = carry[...] + tot
    r8 = lax.broadcasted_iota(jnp.int32, (8, tm), 0)
    mg_ref[...] = jnp.where(r8 < NH_M, g16[0:8], bch[0:8])

    ys = jnp.concatenate(_split3(-LOG2E * c16), axis=0)
    place = lax.dot_general(ys, e48_ref[...], TN, preferred_element_type=F32)
    ones_lo = jnp.where((lane >= DH_F) & (lane < DH_F + 3), 1.0, 0.0)
    ones_hi = jnp.where(lane < 3, 1.0, 0.0)
    zfq = _dot(u, wnat_ref[:, 1024:1536]) * (DH_F ** -0.5 * LOG2E)
    store_groups(fq_ref, groups(zfq, lambda j: ones_lo, lambda j: ones_hi))
    zfk = _dot(u, wnat_ref[:, 1536:2048])
    store_groups(fk_ref, groups(
        zfk,
        lambda j: place[:, LANES * j:LANES * (j + 1)],
        lambda j: place[:, LANES * (j + 4):LANES * (j + 5)]))


def _const_spec(shape):
    nd = len(shape)
    return pl.BlockSpec(shape, lambda *_: (0,) * nd, pipeline_mode=pl.Buffered(1))


def _mix_in(x, n1, wg, wu, wd, nm, wnat, wt, conv, bias16, ucum, ubd, e48):
    B, S, D = x.shape
    n_t = S // TM
    tok = lambda w: pl.BlockSpec((None, TM, w), lambda b, i: (b, i, 0))
    tokt = lambda r: pl.BlockSpec((None, r, TM), lambda b, i: (b, 0, i))
    consts = (n1, wg, wu, wd, nm, wnat, wt, conv, bias16, ucum, ubd, e48)
    out_shape = (
        jax.ShapeDtypeStruct((B, S, D), F32),
        jax.ShapeDtypeStruct((B, S, NH_M * LANES), BF16),
        jax.ShapeDtypeStruct((B, S, NH_M * LANES), BF16),
        jax.ShapeDtypeStruct((B, S, D_MV), BF16),
        jax.ShapeDtypeStruct((B, D_MV, S), BF16),
        jax.ShapeDtypeStruct((B, 8, S), F32),
        jax.ShapeDtypeStruct((B, S, NH_F * LANES), BF16),
        jax.ShapeDtypeStruct((B, S, NH_F * LANES), BF16),
        jax.ShapeDtypeStruct((B, D_FV, S), BF16),
    )
    out_specs = (tok(D), tok(NH_M * LANES), tok(NH_M * LANES), tok(D_MV), tokt(D_MV), tokt(8),
                 tok(NH_F * LANES), tok(NH_F * LANES), tokt(D_FV))
    return pl.pallas_call(
        _mix_in_kernel,
        grid=(B, n_t),
        in_specs=[tok(D)] + [_const_spec(c.shape) for c in consts],
        out_specs=out_specs,
        out_shape=out_shape,
        scratch_shapes=[pltpu.VMEM((TM + 8, 512), F32), pltpu.VMEM((N_GATES, LANES), F32)],
        compiler_params=pltpu.CompilerParams(
            dimension_semantics=("arbitrary", "arbitrary"), vmem_limit_bytes=VMEM_LIMIT),
        name="mix_in",
    )(x, *consts)


def _mlstm_kernel(q_ref, k_ref, vt_ref, mg_ref, mo_ref, gain_ref, o_ref, ct_ref):
    S = q_ref.shape[0]
    L = L_M
    ct_ref[...] = jnp.zeros(ct_ref.shape, F32)
    causal = (lax.broadcasted_iota(jnp.int32, (L, L), 0) <= lax.broadcasted_iota(jnp.int32, (L, L), 1))
    r16 = lax.broadcasted_iota(jnp.int32, (BF16_ROWS, L), 0)
    ones_rows = jnp.where(r16 < 3, 1.0, 0.0).astype(BF16)
    ones16 = jnp.ones((BF16_ROWS, L), BF16)

    def chunk(c, ms):
        cs = pl.multiple_of(c * L, L)
        g8 = mg_ref[:, pl.ds(cs, L)]
        new_ms = []
        for h in range(NH_M):
            hs = slice(LANES * h, LANES * (h + 1))
            m = ms[h]
            logi = g8[h:h + 1]
            b = g8[NH_M + h:NH_M + h + 1]
            a = logi - b
            gtot = b[:, L - 1:L]
            m_new = jnp.maximum(gtot + m, gtot + jnp.max(a, axis=1, keepdims=True))
            decay = jnp.exp(gtot + m - m_new)
            w = jnp.exp(a + gtot - m_new)

            ahi, amid, alo = (p.astype(F32) for p in _split3(a))
            at = jnp.where(r16 == 0, ahi, jnp.where(r16 == 1, amid, jnp.where(r16 == 2, alo, 0.0)))
            acol = lax.dot_general(at.astype(BF16), ones_rows, TN, preferred_element_type=F32)
            e = jnp.where(causal, acol, -jnp.inf)
            mu = jnp.maximum(m, jnp.max(e, axis=0, keepdims=True))

            q = q_ref[pl.ds(cs, L), hs]
            k = k_ref[pl.ds(cs, L), hs]
            st = lax.dot_general(k, q, NT, preferred_element_type=F32)
            pt = (jnp.exp(e - mu) * st).astype(BF16)
            vt_aug = jnp.concatenate([vt_ref[hs, pl.ds(cs, L)], ones16], axis=0)
            ct = ct_ref[h]
            nd = (jnp.exp(m - mu) * lax.dot_general(ct.astype(BF16), q, NT, preferred_element_type=F32)
                  + _dot(vt_aug, pt))
            den = nd[DV_M:DV_M + 1]
            rden = 1.0 / jnp.maximum(jnp.abs(den), jnp.exp(-(b + mu)))
            ht = nd[0:DV_M] * rden
            yt = (ht * lax.rsqrt(jnp.mean(ht * ht, axis=0, keepdims=True) + EPS)
                  * jnp.tile(gain_ref[hs, :], (1, L // LANES)))
            og = jax.nn.sigmoid(mo_ref[pl.ds(cs, L), hs].astype(F32))
            o_ref[pl.ds(cs, L), hs] = (yt.T * og).astype(o_ref.dtype)

            vtw = (vt_aug.astype(F32) * w).astype(BF16)
            ct_ref[h] = decay * ct + _dot(vtw, k)
            new_ms.append(m_new)
        return tuple(new_ms)

    lax.fori_loop(0, S // L, chunk, tuple(jnp.zeros((1, 1), F32) for _ in range(NH_M)))


def _mlstm(mq, mk, mvt, mg, mo, gain):
    B, S, _ = mq.shape
    tok = pl.BlockSpec((None, S, NH_M * LANES), lambda b: (b, 0, 0))
    return pl.pallas_call(
        _mlstm_kernel,
        grid=(B,),
        in_specs=[tok, tok,
                  pl.BlockSpec((None, D_MV, S), lambda b: (b, 0, 0)),
                  pl.BlockSpec((None, 8, S), lambda b: (b, 0, 0)),
                  pl.BlockSpec((None, S, D_MV), lambda b: (b, 0, 0)),
                  _const_spec(gain.shape)],
        out_specs=pl.BlockSpec((None, S, D_MV), lambda b: (b, 0, 0)),
        out_shape=jax.ShapeDtypeStruct((B, S, D_MV), BF16),
        scratch_shapes=[pltpu.VMEM((NH_M, DV_M + BF16_ROWS, LANES), F32)],
        compiler_params=pltpu.CompilerParams(
            dimension_semantics=("arbitrary",), vmem_limit_bytes=VMEM_LIMIT),
        name="mlstm",
    )(mq, mk, mvt, mg, mo, gain)


def _fox_kernel(q_ref, k_ref, vt_ref, gain_ref, o_ref, st_a, st_b, mb_a, mb_b, m_ref, acc_ref, bias_ref):
    S = q_ref.shape[0]
    T = T_F
    n_q = S // T
    heads = [slice(LANES * j, LANES * (j + 1)) for j in range(2)]
    bufs = ((st_a, mb_a), (st_b, mb_b))
    parts = [(j, slice(MXU_COLS * h, MXU_COLS * (h + 1))) for j in range(2) for h in range(T // MXU_COLS)]
    ones16 = jnp.ones((BF16_ROWS, T), BF16)
    visible = lax.broadcasted_iota(jnp.int32, (T, T), 0) <= lax.broadcasted_iota(jnp.int32, (T, T), 1)
    bias_ref[...] = jnp.where(visible, 0.0, NEG)

    def logits(c, buf, diag, part):
        qi, kj = c
        st_buf, mb_buf = buf
        j, cols = part
        qs = pl.multiple_of(qi * T + cols.start, MXU_COLS)
        ks = pl.multiple_of(kj * T, T)
        st = lax.dot_general(k_ref[pl.ds(ks, T), heads[j]], q_ref[pl.ds(qs, MXU_COLS), heads[j]], NT,
                             preferred_element_type=F32)
        if diag:
            st = st + bias_ref[:, cols]
        st_buf[j, :, cols] = st
        mb_buf[j, :, cols] = jnp.max(st, axis=0, keepdims=True)

    def update(c, buf, first, part):
        qi, kj = c
        st_buf, mb_buf = buf
        j, cols = part
        ks = pl.multiple_of(kj * T, T)
        m_old = m_ref[qi, j, :, cols]
        m_new = mb_buf[j, :, cols] if first else jnp.maximum(m_old, mb_buf[j, :, cols])
        pt = jnp.exp2(st_buf[j, :, cols] - m_new).astype(BF16)
        vt_aug = jnp.concatenate([vt_ref[DH_F * j:DH_F * (j + 1), pl.ds(ks, T)], ones16], axis=0)
        pv = _dot(vt_aug, pt)
        acc_ref[qi, j, :, cols] = pv if first else jnp.exp2(m_old - m_new) * acc_ref[qi, j, :, cols] + pv
        m_ref[qi, j, :, cols] = m_new

    def step(cur, nxt, b, diag):
        for part in parts:
            if nxt is not None:
                logits(nxt, bufs[1 - b], diag, part)
            update(cur, bufs[b], diag, part)

    def sweep(n, start, succ, diag):
        if n == 0:
            return
        for part in parts:
            logits(start, bufs[0], diag, part)

        def trip(_, c):
            for i in range(FOX_UNROLL):
                cn = succ(c)
                step(c, cn, i % 2, diag)
                c = cn
            return c

        trips = (n - 1) // FOX_UNROLL
        c = lax.fori_loop(0, trips, trip, start)
        rest = n - trips * FOX_UNROLL
        for i in range(rest):
            cn = succ(c) if i < rest - 1 else None
            step(c, cn, i % 2, diag)
            c = cn

    zero = jnp.int32(0)
    sweep(n_q, (zero, zero), lambda c: (c[0] + 1, c[1] + 1), True)

    def below(c):
        wrap = c[1] + 1 >= c[0]
        return jnp.where(wrap, c[0] + 1, c[0]), jnp.where(wrap, 0, c[1] + 1)

    sweep(n_q * (n_q - 1) // 2, (zero + 1, zero), below, False)

    def finish(qi, _):
        ys = []
        for j in range(2):
            acc = acc_ref[qi, j]
            o = acc[0:DH_F] * (1.0 / acc[DH_F:DH_F + 1])
            ys.append(o * lax.rsqrt(jnp.mean(o * o, axis=0, keepdims=True) + EPS)
                      * jnp.tile(gain_ref[DH_F * j:DH_F * (j + 1), :], (1, T // LANES)))
        o_ref[pl.ds(pl.multiple_of(qi * T, T), T), :] = jnp.concatenate(ys, axis=0).T.astype(o_ref.dtype)
        return 0

    lax.fori_loop(0, n_q, finish, 0)


def _fox(fq, fk, fvt, gain):
    B, S, _ = fq.shape
    n_pair = NH_F // 2
    tok = pl.BlockSpec((None, S, 2 * LANES), lambda b, p: (b, 0, p))
    return pl.pallas_call(
        _fox_kernel,
        grid=(B, n_pair),
        in_specs=[tok, tok,
                  pl.BlockSpec((None, 2 * DH_F, S), lambda b, p: (b, p, 0)),
                  pl.BlockSpec((2 * DH_F, LANES), lambda b, p: (p, 0))],
        out_specs=pl.BlockSpec((None, S, 2 * DH_F), lambda b, p: (b, 0, p)),
        out_shape=jax.ShapeDtypeStruct((B, S, D_FV), BF16),
        scratch_shapes=[pltpu.VMEM((2, T_F, T_F), F32), pltpu.VMEM((2, T_F, T_F), F32),
                        pltpu.VMEM((2, 1, T_F), F32), pltpu.VMEM((2, 1, T_F), F32),
                        pltpu.VMEM((S // T_F, 2, 1, T_F), F32),
                        pltpu.VMEM((S // T_F, 2, DH_F + BF16_ROWS, T_F), F32),
                        pltpu.VMEM((T_F, T_F), F32)],
        compiler_params=pltpu.CompilerParams(
            dimension_semantics=("arbitrary", "arbitrary"), vmem_limit_bytes=VMEM_LIMIT),
        name="fox",
    )(fq, fk, fvt, gain)


def _mix_out_kernel(h_ref, ym_ref, yf_ref, p_ref, wom_ref, wof_ref, n2_ref, wg_ref, wu_ref, wd_ref,
                    ng_ref, wpg_ref, wpp_ref, npp_ref, nf_ref, o_ref):
    h = h_ref[...] + _dot(ym_ref[...], wom_ref[...]) + _dot(yf_ref[...], wof_ref[...])
    h = h + 0.5 * _swiglu(h, n2_ref, wg_ref, wu_ref, wd_ref)
    gate = jax.nn.sigmoid(_dot(_rms(h, ng_ref[...]).astype(BF16), wpg_ref[...]))
    pe = _rms(_dot(p_ref[...].astype(BF16), wpp_ref[...]), npp_ref[...])
    h = h + gate * pe
    o_ref[...] = _rms(h, nf_ref[...])


def _mix_out(h1, ym, yf, p, *consts):
    B, S, D = h1.shape
    tok = lambda w: pl.BlockSpec((None, TM, w), lambda b, i: (b, i, 0))
    return pl.pallas_call(
        _mix_out_kernel,
        grid=(B, S // TM),
        in_specs=[tok(D), tok(D_MV), tok(D_FV), tok(p.shape[-1])] + [_const_spec(c.shape) for c in consts],
        out_specs=tok(D),
        out_shape=jax.ShapeDtypeStruct((B, S, D), F32),
        compiler_params=pltpu.CompilerParams(
            dimension_semantics=("arbitrary", "arbitrary"), vmem_limit_bytes=VMEM_LIMIT),
        name="mix_out",
    )(h1, ym, yf, p, *consts)


def _interleave(base, n_heads, width):
    half = n_heads // 2
    idx = []
    for j in range(half):
        idx += list(range(base + j * width, base + (j + 1) * width))
        idx += list(range(base + (j + half) * width, base + (j + half + 1) * width))
    return idx


_O_MQ, _O_MK, _O_MV, _O_MO, _O_MIF = 0, 256, 512, 1024, 1536
_O_FQ, _O_FK, _O_FV, _O_FF = 1544, 2056, 2568, 3080
_QK_PERM = np.array(_interleave(0, NH_M, DK_M) + _interleave(256, NH_M, DK_M))
_NAT_COLS = np.concatenate([
    _QK_PERM, np.arange(_O_MO, _O_MO + D_MV),
    np.array(_interleave(_O_FQ, NH_F, DH_F)), np.array(_interleave(_O_FK, NH_F, DH_F))])
_T_COLS = np.concatenate([np.arange(_O_MV, _O_MV + D_MV), np.arange(_O_FV, _O_FV + D_FV),
                          np.arange(_O_MIF, _O_MIF + 2 * NH_M), np.arange(_O_FF, _O_FF + NH_F)])


def _placement():
    e = np.zeros((3 * N_GATES, NH_F * LANES), np.float32)
    for j in range(3):
        for h in range(NH_F):
            e[N_GATES * j + 2 * NH_M + h, LANES * h + (DH_F + j if h < NH_F // 2 else j)] = 1.0
    return e


def _tri(n, block):
    s = np.arange(n)[:, None]
    t = np.arange(n)[None, :]
    return ((s <= t) & (s // block == t // block)).astype(np.float32)


def kernel(x, p, ffn1_norm, ffn1_w_gate, ffn1_w_up, ffn1_w_down, mix_norm, w_in, conv_qk, b_mlstm_gates, b_fox_f, mlstm_out_norm, fox_out_norm, w_out, ffn2_norm, ffn2_w_gate, ffn2_w_up, ffn2_w_down, ple_gate_norm, w_ple_gate, w_ple_proj, ple_proj_norm, final_norm):
    depth = w_in.shape[0]
    assert depth == 1, "the final RMSNorm is fused into the (single) layer's last kernel"
    assert x.shape[1] % TM == 0 and x.shape[2] == D_MODEL
    row = lambda v: v.reshape(1, -1).astype(F32)
    col = lambda v: jnp.broadcast_to(v.astype(F32)[:, None], (v.shape[0], LANES))
    ucum = jnp.asarray(_tri(TM, TM), BF16)
    ubd = jnp.asarray(_tri(TM, L_M), BF16)
    e48 = jnp.asarray(_placement(), BF16)
    h = x
    for i in range(depth):
        wnat = w_in[i][:, _NAT_COLS].astype(BF16)
        wt = w_in[i][:, _T_COLS].T.astype(BF16)
        conv = conv_qk[i][:, _QK_PERM].astype(F32)
        bias16 = col(jnp.concatenate([b_mlstm_gates[i], b_fox_f[i]]))
        h1, mq, mk, mo, mvt, mg, fq, fk, fvt = _mix_in(
            h, row(ffn1_norm[i]), ffn1_w_gate[i].astype(BF16), ffn1_w_up[i].astype(BF16),
            ffn1_w_down[i].astype(BF16), row(mix_norm[i]), wnat, wt, conv, bias16, ucum, ubd, e48)
        ym = _mlstm(mq, mk, mvt, mg, mo, col(mlstm_out_norm[i]))
        yf = _fox(fq, fk, fvt, col(fox_out_norm[i]))
        wo = w_out[i].astype(BF16)
        h = _mix_out(
            h1, ym, yf, p[i], wo[:D_MV], wo[D_MV:], row(ffn2_norm[i]), ffn2_w_gate[i].astype(BF16),
            ffn2_w_up[i].astype(BF16), ffn2_w_down[i].astype(BF16), row(ple_gate_norm[i]),
            w_ple_gate[i].astype(BF16), w_ple_proj[i].astype(BF16), row(ple_proj_norm[i]),
            row(final_norm))
    return h
```

```python
import numpy as np
import jax
import jax.numpy as jnp
from jax import lax
from jax.experimental import pallas as pl
from jax.experimental.pallas import tpu as pltpu

F32 = jnp.float32
BF16 = jnp.bfloat16

EPS = 1e-6
D_MODEL = 1024
D_FF = 2816
NH_M = 4
DV_M = 128
DK_M = 64
CONV_W = 4
NH_F = 8
DH_F = 64
D_MV = NH_M * DV_M
D_FV = NH_F * DH_F
N_GATES = 2 * NH_M + NH_F

LANES = 128
BF16_ROWS = 16
MXU_COLS = 256
TM_IN = 512
TM_OUT = 1024
UNIT = 512
FF_CHUNK = 256
N_FF_CHUNKS = D_FF // FF_CHUNK
L_M = 256
MLSTM_CHUNKS = 2
T_F = 512
LOG2E = 1.4426950408889634
FOX_UNROLL = 4
NEG = -1e30
VMEM_LIMIT = 60 * 1024 * 1024

NT = (((1,), (1,)), ((), ()))
TN = (((0,), (0,)), ((), ()))


def _rms(x, g):
    return x * lax.rsqrt(jnp.mean(x * x, axis=-1, keepdims=True) + EPS) * g


def _dot(a, b):
    return jnp.dot(a, b, preferred_element_type=F32)


def _split3(x):
    hi = x.astype(BF16)
    r = x - hi.astype(F32)
    mid = r.astype(BF16)
    lo = (r - mid.astype(F32)).astype(BF16)
    return hi, mid, lo


def _swiglu_chunk(xn, c, wg_ref, wu_ref, wd_ref):
    a = _dot(xn, wg_ref[:, c:c + FF_CHUNK])
    b = _dot(xn, wu_ref[:, c:c + FF_CHUNK])
    return _dot((a * jax.nn.sigmoid(a) * b).astype(BF16), wd_ref[c:c + FF_CHUNK, :])


def _skew(units, lag, early):
    done = [0] * len(units)
    tick = 0
    while any(d >= 0 for d in done):
        for i, u in enumerate(units):
            due = max(0, i * lag - early) if done[i] == 0 else i * lag + done[i]
            if done[i] >= 0 and tick >= due:
                try:
                    next(u)
                    done[i] += 1
                except StopIteration:
                    done[i] = -1
        tick += 1


def _mix_in_kernel(x_ref, n1_ref, wg_ref, wu_ref, wd_ref, nm_ref, wnat_ref, wt_ref, conv_ref,
                   bias_ref, ucum_ref, ubd_ref, e48_ref,
                   h1_ref, mq_ref, mk_ref, mo_ref, mvt_ref, mg_ref, fq_ref, fk_ref, fvt_ref,
                   cbuf, carry):
    tm = x_ref.shape[0]
    n_units = tm // UNIT

    @pl.when(pl.program_id(1) == 0)
    def _():
        cbuf[0:8, :] = jnp.zeros((8, cbuf.shape[1]), F32)
        carry[...] = jnp.zeros(carry.shape, F32)

    lane = lax.broadcasted_iota(jnp.int32, (UNIT, LANES), 1)
    low = lane < DK_M
    ones_lo = jnp.where((lane >= DH_F) & (lane < DH_F + 3), 1.0, 0.0)
    ones_hi = jnp.where(lane < 3, 1.0, 0.0)
    r8 = lax.broadcasted_iota(jnp.int32, (8, UNIT), 0)

    def groups(z, lo_fill, hi_fill):
        cols = [z[:, LANES * j:LANES * (j + 1)] for j in range(z.shape[1] // LANES)]
        return ([jnp.where(low, c, lo_fill(j)) for j, c in enumerate(cols)]
                + [jnp.where(low, hi_fill(j), c) for j, c in enumerate(cols)])

    def store_groups(ref, rows, gs):
        for g, v in enumerate(gs):
            ref[rows, LANES * g:LANES * (g + 1)] = v.astype(ref.dtype)

    def fold(r):
        return r[0:N_GATES] + r[N_GATES:2 * N_GATES] + r[2 * N_GATES:3 * N_GATES]

    zero = lambda j: 0.0

    def unit(i):
        r0 = i * UNIT
        rows = slice(r0, r0 + UNIT)
        x = x_ref[rows, :]
        xn = _rms(x, n1_ref[...]).astype(BF16)
        yield
        acc = None
        for c in range(0, D_FF, FF_CHUNK):
            part = _swiglu_chunk(xn, c, wg_ref, wu_ref, wd_ref)
            acc = part if acc is None else acc + part
            yield
        h1 = x + 0.5 * acc
        h1_ref[rows, :] = h1
        u = _rms(h1, nm_ref[...]).astype(BF16)
        yield

        zqk = _dot(u, wnat_ref[:, 0:512])
        cbuf[8 + r0:8 + r0 + UNIT, :] = zqk
        yield
        cv = conv_ref[CONV_W - 1:CONV_W, :] * zqk
        for j in range(CONV_W - 1):
            top = 8 + r0 - (CONV_W - 1) + j
            cv = cv + conv_ref[j:j + 1, :] * cbuf[top:top + UNIT, :]
        if i == n_units - 1:
            cbuf[0:8, :] = cbuf[tm:tm + 8, :]
        s = cv * jax.nn.sigmoid(cv)
        store_groups(mq_ref, rows, groups(s[:, 0:256] * (DK_M ** -0.5), zero, zero))
        store_groups(mk_ref, rows, groups(s[:, 256:512], zero, zero))
        yield

        mo_ref[rows, :] = _dot(u, wnat_ref[:, 512:1024]).astype(BF16)
        yield

        zt = lax.dot_general(wt_ref[...], u, NT, preferred_element_type=F32)
        mvt_ref[:, rows] = zt[0:D_MV].astype(BF16)
        fvt_ref[:, rows] = zt[D_MV:D_MV + D_FV].astype(BF16)
        g16 = zt[D_MV + D_FV:D_MV + D_FV + N_GATES] + jnp.tile(bias_ref[...], (1, UNIT // LANES))
        yield
        lf = jnp.minimum(g16, 0.0) - jnp.log1p(jnp.exp(-jnp.abs(g16)))
        xs = jnp.concatenate(_split3(lf), axis=0)
        cum = fold(_dot(xs, ucum_ref[...]))
        bch = fold(_dot(xs, ubd_ref[...]))
        tot = fold(_dot(xs, jnp.ones((UNIT, LANES), BF16)))
        c16 = cum + jnp.tile(carry[...], (1, UNIT // LANES))
        carry[...] = carry[...] + tot
        mg_ref[:, rows] = jnp.where(r8 < NH_M, g16[0:8], bch[0:8])
        ys = jnp.concatenate(_split3(-LOG2E * c16), axis=0)
        place = lax.dot_general(ys, e48_ref[...], TN, preferred_element_type=F32)
        yield
        zfq = _dot(u, wnat_ref[:, 1024:1536]) * (DH_F ** -0.5 * LOG2E)
        store_groups(fq_ref, rows, groups(zfq, lambda j: ones_lo, lambda j: ones_hi))
        yield
        zfk = _dot(u, wnat_ref[:, 1536:2048])
        store_groups(fk_ref, rows, groups(
            zfk,
            lambda j: place[:, LANES * j:LANES * (j + 1)],
            lambda j: place[:, LANES * (j + 4):LANES * (j + 5)]))
        yield

    _skew([unit(i) for i in range(n_units)], lag=N_FF_CHUNKS + 1, early=N_FF_CHUNKS - 2)


def _const_spec(shape):
    nd = len(shape)
    return pl.BlockSpec(shape, lambda *_: (0,) * nd, pipeline_mode=pl.Buffered(1))


def _mix_in(x, n1, wg, wu, wd, nm, wnat, wt, conv, bias16, ucum, ubd, e48):
    B, S, D = x.shape
    n_t = S // TM_IN
    tok = lambda w: pl.BlockSpec((None, TM_IN, w), lambda b, i: (b, i, 0))
    tokt = lambda r: pl.BlockSpec((None, r, TM_IN), lambda b, i: (b, 0, i))
    consts = (n1, wg, wu, wd, nm, wnat, wt, conv, bias16, ucum, ubd, e48)
    out_shape = (
        jax.ShapeDtypeStruct((B, S, D), F32),
        jax.ShapeDtypeStruct((B, S, NH_M * LANES), BF16),
        jax.ShapeDtypeStruct((B, S, NH_M * LANES), BF16),
        jax.ShapeDtypeStruct((B, S, D_MV), BF16),
        jax.ShapeDtypeStruct((B, D_MV, S), BF16),
        jax.ShapeDtypeStruct((B, 8, S), F32),
        jax.ShapeDtypeStruct((B, S, NH_F * LANES), BF16),
        jax.ShapeDtypeStruct((B, S, NH_F * LANES), BF16),
        jax.ShapeDtypeStruct((B, D_FV, S), BF16),
    )
    out_specs = (tok(D), tok(NH_M * LANES), tok(NH_M * LANES), tok(D_MV), tokt(D_MV), tokt(8),
                 tok(NH_F * LANES), tok(NH_F * LANES), tokt(D_FV))
    return pl.pallas_call(
        _mix_in_kernel,
        grid=(B, n_t),
        in_specs=[tok(D)] + [_const_spec(c.shape) for c in consts],
        out_specs=out_specs,
        out_shape=out_shape,
        scratch_shapes=[pltpu.VMEM((TM_IN + 8, 512), F32), pltpu.VMEM((N_GATES, LANES), F32)],
        compiler_params=pltpu.CompilerParams(
            dimension_semantics=("arbitrary", "arbitrary"), vmem_limit_bytes=VMEM_LIMIT),
        name="mix_in",
    )(x, *consts)


def _mlstm_kernel(q_ref, k_ref, vt_ref, mg_ref, mo_ref, gain_ref, o_ref, ct_ref):
    S = q_ref.shape[0]
    L = L_M
    ct_ref[...] = jnp.zeros(ct_ref.shape, F32)
    causal = (lax.broadcasted_iota(jnp.int32, (L, L), 0) <= lax.broadcasted_iota(jnp.int32, (L, L), 1))
    r16 = lax.broadcasted_iota(jnp.int32, (BF16_ROWS, L), 0)
    ones_rows = jnp.where(r16 < 3, 1.0, 0.0).astype(BF16)
    ones16 = jnp.ones((BF16_ROWS, L), BF16)

    def head_chunk(cs, h, ms):
        hs = slice(LANES * h, LANES * (h + 1))
        m = ms[h]
        logi = mg_ref[h:h + 1, pl.ds(cs, L)]
        b = mg_ref[NH_M + h:NH_M + h + 1, pl.ds(cs, L)]
        a = logi - b
        gtot = b[:, L - 1:L]
        m_new = jnp.maximum(gtot + m, gtot + jnp.max(a, axis=1, keepdims=True))
        ms[h] = m_new
        ahi, amid, alo = (p.astype(F32) for p in _split3(a))
        at = jnp.where(r16 == 0, ahi, jnp.where(r16 == 1, amid, jnp.where(r16 == 2, alo, 0.0)))
        acol = lax.dot_general(at.astype(BF16), ones_rows, TN, preferred_element_type=F32)
        q = q_ref[pl.ds(cs, L), hs]
        k = k_ref[pl.ds(cs, L), hs]
        st = lax.dot_general(k, q, NT, preferred_element_type=F32)
        ct = ct_ref[h]
        inter = lax.dot_general(ct.astype(BF16), q, NT, preferred_element_type=F32)
        yield
        e = jnp.where(causal, acol, -jnp.inf)
        mu = jnp.maximum(m, jnp.max(e, axis=0, keepdims=True))
        pt = (jnp.exp(e - mu) * st).astype(BF16)
        vt_aug = jnp.concatenate([vt_ref[hs, pl.ds(cs, L)], ones16], axis=0)
        w = jnp.exp(a + gtot - m_new)
        vtw = (vt_aug.astype(F32) * w).astype(BF16)
        yield
        nd = jnp.exp(m - mu) * inter + _dot(vt_aug, pt)
        ct_ref[h] = jnp.exp(gtot + m - m_new) * ct + _dot(vtw, k)
        yield
        rden = 1.0 / jnp.maximum(jnp.abs(nd[DV_M:DV_M + 1]), jnp.exp(-(b + mu)))
        ht = nd[0:DV_M] * rden
        yt = (ht * lax.rsqrt(jnp.mean(ht * ht, axis=0, keepdims=True) + EPS)
              * jnp.tile(gain_ref[hs, :], (1, L // LANES)))
        og = jax.nn.sigmoid(mo_ref[pl.ds(cs, L), hs].astype(F32))
        o_ref[pl.ds(cs, L), hs] = (yt.T * og).astype(o_ref.dtype)
        yield

    def trip(c, ms):
        ms = list(ms)
        starts = [pl.multiple_of((c * MLSTM_CHUNKS + j) * L, L) for j in range(MLSTM_CHUNKS)]
        _skew([head_chunk(cs, h, ms) for cs in starts for h in range(NH_M)], lag=1, early=0)
        return tuple(ms)

    lax.fori_loop(0, S // (L * MLSTM_CHUNKS), trip, tuple(jnp.zeros((1, 1), F32) for _ in range(NH_M)))


def _mlstm(mq, mk, mvt, mg, mo, gain):
    B, S, _ = mq.shape
    tok = pl.BlockSpec((None, S, NH_M * LANES), lambda b: (b, 0, 0))
    return pl.pallas_call(
        _mlstm_kernel,
        grid=(B,),
        in_specs=[tok, tok,
                  pl.BlockSpec((None, D_MV, S), lambda b: (b, 0, 0)),
                  pl.BlockSpec((None, 8, S), lambda b: (b, 0, 0)),
                  pl.BlockSpec((None, S, D_MV), lambda b: (b, 0, 0)),
                  _const_spec(gain.shape)],
        out_specs=pl.BlockSpec((None, S, D_MV), lambda b: (b, 0, 0)),
        out_shape=jax.ShapeDtypeStruct((B, S, D_MV), BF16),
        scratch_shapes=[pltpu.VMEM((NH_M, DV_M + BF16_ROWS, LANES), F32)],
        compiler_params=pltpu.CompilerParams(
            dimension_semantics=("arbitrary",), vmem_limit_bytes=VMEM_LIMIT),
        name="mlstm",
    )(mq, mk, mvt, mg, mo, gain)


def _fox_kernel(q_ref, k_ref, vt_ref, gain_ref, o_ref, st_a, st_b, mb_a, mb_b, m_ref, acc_ref, bias_ref):
    S = q_ref.shape[0]
    T = T_F
    n_q = S // T
    heads = [slice(LANES * j, LANES * (j + 1)) for j in range(2)]
    bufs = ((st_a, mb_a), (st_b, mb_b))
    parts = [(j, slice(MXU_COLS * h, MXU_COLS * (h + 1))) for j in range(2) for h in range(T // MXU_COLS)]
    ones16 = jnp.ones((BF16_ROWS, T), BF16)
    visible = lax.broadcasted_iota(jnp.int32, (T, T), 0) <= lax.broadcasted_iota(jnp.int32, (T, T), 1)
    bias_ref[...] = jnp.where(visible, 0.0, NEG)

    def logits(c, buf, diag, part):
        qi, kj = c
        st_buf, mb_buf = buf
        j, cols = part
        qs = pl.multiple_of(qi * T + cols.start, MXU_COLS)
        ks = pl.multiple_of(kj * T, T)
        st = lax.dot_general(k_ref[pl.ds(ks, T), heads[j]], q_ref[pl.ds(qs, MXU_COLS), heads[j]], NT,
                             preferred_element_type=F32)
        if diag:
            st = st + bias_ref[:, cols]
        st_buf[j, :, cols] = st
        mb_buf[j, :, cols] = jnp.max(st, axis=0, keepdims=True)

    def update(c, buf, first, part):
        qi, kj = c
        st_buf, mb_buf = buf
        j, cols = part
        ks = pl.multiple_of(kj * T, T)
        m_old = m_ref[qi, j, :, cols]
        m_new = mb_buf[j, :, cols] if first else jnp.maximum(m_old, mb_buf[j, :, cols])
        pt = jnp.exp2(st_buf[j, :, cols] - m_new).astype(BF16)
        vt_aug = jnp.concatenate([vt_ref[DH_F * j:DH_F * (j + 1), pl.ds(ks, T)], ones16], axis=0)
        pv = _dot(vt_aug, pt)
        acc_ref[qi, j, :, cols] = pv if first else jnp.exp2(m_old - m_new) * acc_ref[qi, j, :, cols] + pv
        m_ref[qi, j, :, cols] = m_new

    def step(cur, nxt, b, diag):
        for part in parts:
            if nxt is not None:
                logits(nxt, bufs[1 - b], diag, part)
            update(cur, bufs[b], diag, part)

    def sweep(n, start, succ, diag):
        if n == 0:
            return
        for part in parts:
            logits(start, bufs[0], diag, part)

        def trip(_, c):
            for i in range(FOX_UNROLL):
                cn = succ(c)
                step(c, cn, i % 2, diag)
                c = cn
            return c

        trips = (n - 1) // FOX_UNROLL
        c = lax.fori_loop(0, trips, trip, start)
        rest = n - trips * FOX_UNROLL
        for i in range(rest):
            cn = succ(c) if i < rest - 1 else None
            step(c, cn, i % 2, diag)
            c = cn

    zero = jnp.int32(0)
    sweep(n_q, (zero, zero), lambda c: (c[0] + 1, c[1] + 1), True)

    def below(c):
        wrap = c[1] + 1 >= c[0]
        return jnp.where(wrap, c[0] + 1, c[0]), jnp.where(wrap, 0, c[1] + 1)

    sweep(n_q * (n_q - 1) // 2, (zero + 1, zero), below, False)

    def finish(qi, _):
        ys = []
        for j in range(2):
            acc = acc_ref[qi, j]
            o = acc[0:DH_F] * (1.0 / acc[DH_F:DH_F + 1])
            ys.append(o * lax.rsqrt(jnp.mean(o * o, axis=0, keepdims=True) + EPS)
                      * jnp.tile(gain_ref[DH_F * j:DH_F * (j + 1), :], (1, T // LANES)))
        o_ref[pl.ds(pl.multiple_of(qi * T, T), T), :] = jnp.concatenate(ys, axis=0).T.astype(o_ref.dtype)
        return 0

    lax.fori_loop(0, n_q, finish, 0)


def _fox(fq, fk, fvt, gain):
    B, S, _ = fq.shape
    n_pair = NH_F // 2
    tok = pl.BlockSpec((None, S, 2 * LANES), lambda b, p: (b, 0, p))
    return pl.pallas_call(
        _fox_kernel,
        grid=(B, n_pair),
        in_specs=[tok, tok,
                  pl.BlockSpec((None, 2 * DH_F, S), lambda b, p: (b, p, 0)),
                  pl.BlockSpec((2 * DH_F, LANES), lambda b, p: (p, 0))],
        out_specs=pl.BlockSpec((None, S, 2 * DH_F), lambda b, p: (b, 0, p)),
        out_shape=jax.ShapeDtypeStruct((B, S, D_FV), BF16),
        scratch_shapes=[pltpu.VMEM((2, T_F, T_F), F32), pltpu.VMEM((2, T_F, T_F), F32),
                        pltpu.VMEM((2, 1, T_F), F32), pltpu.VMEM((2, 1, T_F), F32),
                        pltpu.VMEM((S // T_F, 2, 1, T_F), F32),
                        pltpu.VMEM((S // T_F, 2, DH_F + BF16_ROWS, T_F), F32),
                        pltpu.VMEM((T_F, T_F), F32)],
        compiler_params=pltpu.CompilerParams(
            dimension_semantics=("arbitrary", "arbitrary"), vmem_limit_bytes=VMEM_LIMIT),
        name="fox",
    )(fq, fk, fvt, gain)


def _mix_out_kernel(h_ref, ym_ref, yf_ref, p_ref, wom_ref, wof_ref, n2_ref, wg_ref, wu_ref, wd_ref,
                    ng_ref, wpg_ref, wpp_ref, npp_ref, nf_ref, o_ref):
    def unit(i):
        rows = slice(i * UNIT, (i + 1) * UNIT)
        h = h_ref[rows, :] + _dot(ym_ref[rows, :], wom_ref[...]) + _dot(yf_ref[rows, :], wof_ref[...])
        pe = _rms(_dot(p_ref[rows, :].astype(BF16), wpp_ref[...]), npp_ref[...])
        xn = _rms(h, n2_ref[...]).astype(BF16)
        yield
        acc = None
        for c in range(0, D_FF, FF_CHUNK):
            part = _swiglu_chunk(xn, c, wg_ref, wu_ref, wd_ref)
            acc = part if acc is None else acc + part
            yield
        h = h + 0.5 * acc
        xg = _rms(h, ng_ref[...]).astype(BF16)
        yield
        h = h + jax.nn.sigmoid(_dot(xg, wpg_ref[...])) * pe
        o_ref[rows, :] = _rms(h, nf_ref[...])
        yield

    _skew([unit(i) for i in range(h_ref.shape[0] // UNIT)], lag=N_FF_CHUNKS + 1, early=N_FF_CHUNKS - 2)


def _mix_out(h1, ym, yf, p, *consts):
    B, S, D = h1.shape
    tok = lambda w: pl.BlockSpec((None, TM_OUT, w), lambda b, i: (b, i, 0))
    return pl.pallas_call(
        _mix_out_kernel,
        grid=(B, S // TM_OUT),
        in_specs=[tok(D), tok(D_MV), tok(D_FV), tok(p.shape[-1])] + [_const_spec(c.shape) for c in consts],
        out_specs=tok(D),
        out_shape=jax.ShapeDtypeStruct((B, S, D), F32),
        compiler_params=pltpu.CompilerParams(
            dimension_semantics=("arbitrary", "arbitrary"), vmem_limit_bytes=VMEM_LIMIT),
        name="mix_out",
    )(h1, ym, yf, p, *consts)


def _interleave(base, n_heads, width):
    half = n_heads // 2
    idx = []
    for j in range(half):
        idx += list(range(base + j * width, base + (j + 1) * width))
        idx += list(range(base + (j + half) * width, base + (j + half + 1) * width))
    return idx


_O_MQ, _O_MK, _O_MV, _O_MO, _O_MIF = 0, 256, 512, 1024, 1536
_O_FQ, _O_FK, _O_FV, _O_FF = 1544, 2056, 2568, 3080
_QK_PERM = np.array(_interleave(0, NH_M, DK_M) + _interleave(256, NH_M, DK_M))
_NAT_COLS = np.concatenate([
    _QK_PERM, np.arange(_O_MO, _O_MO + D_MV),
    np.array(_interleave(_O_FQ, NH_F, DH_F)), np.array(_interleave(_O_FK, NH_F, DH_F))])
_T_COLS = np.concatenate([np.arange(_O_MV, _O_MV + D_MV), np.arange(_O_FV, _O_FV + D_FV),
                          np.arange(_O_MIF, _O_MIF + 2 * NH_M), np.arange(_O_FF, _O_FF + NH_F)])


def _placement():
    e = np.zeros((3 * N_GATES, NH_F * LANES), np.float32)
    for j in range(3):
        for h in range(NH_F):
            e[N_GATES * j + 2 * NH_M + h, LANES * h + (DH_F + j if h < NH_F // 2 else j)] = 1.0
    return e


def _tri(n, block):
    s = np.arange(n)[:, None]
    t = np.arange(n)[None, :]
    return ((s <= t) & (s // block == t // block)).astype(np.float32)


def _take_cols(w, cols):
    cuts = [0] + [i for i in range(1, len(cols)) if cols[i] != cols[i - 1] + 1] + [len(cols)]
    return jnp.concatenate([w[:, int(cols[a]):int(cols[b - 1]) + 1] for a, b in zip(cuts[:-1], cuts[1:])], axis=1)


def kernel(x, p, ffn1_norm, ffn1_w_gate, ffn1_w_up, ffn1_w_down, mix_norm, w_in, conv_qk, b_mlstm_gates, b_fox_f, mlstm_out_norm, fox_out_norm, w_out, ffn2_norm, ffn2_w_gate, ffn2_w_up, ffn2_w_down, ple_gate_norm, w_ple_gate, w_ple_proj, ple_proj_norm, final_norm):
    depth = w_in.shape[0]
    assert depth == 1, "the final RMSNorm is fused into the (single) layer's last kernel"
    assert x.shape[1] % TM_OUT == 0 and x.shape[2] == D_MODEL
    assert TM_IN % UNIT == 0 and TM_OUT % UNIT == 0 and UNIT % L_M == 0 and FOX_UNROLL % 2 == 0
    row = lambda v: v.reshape(1, -1).astype(F32)
    col = lambda v: jnp.broadcast_to(v.astype(F32)[:, None], (v.shape[0], LANES))
    ucum = jnp.asarray(_tri(UNIT, UNIT), BF16)
    ubd = jnp.asarray(_tri(UNIT, L_M), BF16)
    e48 = jnp.asarray(_placement(), BF16)
    h = x
    for i in range(depth):
        wnat = _take_cols(w_in[i], _NAT_COLS).astype(BF16)
        wt = _take_cols(w_in[i], _T_COLS).T.astype(BF16)
        conv = _take_cols(conv_qk[i], _QK_PERM).astype(F32)
        bias16 = col(jnp.concatenate([b_mlstm_gates[i], b_fox_f[i]]))
        h1, mq, mk, mo, mvt, mg, fq, fk, fvt = _mix_in(
            h, row(ffn1_norm[i]), ffn1_w_gate[i].astype(BF16), ffn1_w_up[i].astype(BF16),
            ffn1_w_down[i].astype(BF16), row(mix_norm[i]), wnat, wt, conv, bias16, ucum, ubd, e48)
        ym = _mlstm(mq, mk, mvt, mg, mo, col(mlstm_out_norm[i]))
        yf = _fox(fq, fk, fvt, col(fox_out_norm[i]))
        wo = w_out[i].astype(BF16)
        h = _mix_out(
            h1, ym, yf, p[i], wo[:D_MV], wo[D_MV:], row(ffn2_norm[i]), ffn2_w_gate[i].astype(BF16),
            ffn2_w_up[i].astype(BF16), ffn2_w_down[i].astype(BF16), row(ple_gate_norm[i]),
            w_ple_gate[i].astype(BF16), w_ple_proj[i].astype(BF16), row(ple_proj_norm[i]),
            row(final_norm))
    return h
```

```python
import numpy as np
import jax
import jax.numpy as jnp
from jax import lax
from jax.experimental import pallas as pl
from jax.experimental.pallas import tpu as pltpu

F32 = jnp.float32
BF16 = jnp.bfloat16

EPS = 1e-6
D_MODEL = 1024
D_FF = 2816
NH_M = 4
DV_M = 128
DK_M = 64
CONV_W = 4
NH_F = 8
DH_F = 64
D_MV = NH_M * DV_M
D_FV = NH_F * DH_F
N_GATES = 2 * NH_M + NH_F

LANES = 128
BF16_ROWS = 16
MXU_COLS = 256
TM_IN = 512
TM_OUT = 1024
UNIT = 512
FF_CHUNK = 256
N_FF_CHUNKS = D_FF // FF_CHUNK
L_M = 256
MLSTM_CHUNKS = 2
T_F = 512
LOG2E = 1.4426950408889634
FOX_UNROLL = 4
NEG = -1e30
VMEM_LIMIT = 60 * 1024 * 1024

NT = (((1,), (1,)), ((), ()))
TN = (((0,), (0,)), ((), ()))


def _rms(x, g):
    return x * lax.rsqrt(jnp.mean(x * x, axis=-1, keepdims=True) + EPS) * g


def _dot(a, b):
    return jnp.dot(a, b, preferred_element_type=F32)


def _split3(x):
    hi = x.astype(BF16)
    r = x - hi.astype(F32)
    mid = r.astype(BF16)
    lo = (r - mid.astype(F32)).astype(BF16)
    return hi, mid, lo


def _swiglu_chunk(xn, c, wg_ref, wu_ref, wd_ref):
    a = _dot(xn, wg_ref[:, c:c + FF_CHUNK])
    b = _dot(xn, wu_ref[:, c:c + FF_CHUNK])
    return _dot((a * jax.nn.sigmoid(a) * b).astype(BF16), wd_ref[c:c + FF_CHUNK, :])


def _skew(units, lag, early):
    done = [0] * len(units)
    tick = 0
    while any(d >= 0 for d in done):
        for i, u in enumerate(units):
            due = max(0, i * lag - early) if done[i] == 0 else i * lag + done[i]
            if done[i] >= 0 and tick >= due:
                try:
                    next(u)
                    done[i] += 1
                except StopIteration:
                    done[i] = -1
        tick += 1


def _mix_in_kernel(x_ref, n1_ref, wg_ref, wu_ref, wd_ref, nm_ref, wnat_ref, wt_ref, conv_ref,
                   bias_ref, ucum_ref, ubd_ref, e48_ref,
                   h1_ref, mq_ref, mk_ref, mo_ref, mvt_ref, mg_ref, fq_ref, fk_ref, fvt_ref,
                   cbuf, carry):
    tm = x_ref.shape[0]
    n_units = tm // UNIT

    @pl.when(pl.program_id(1) == 0)
    def _():
        cbuf[0:8, :] = jnp.zeros((8, cbuf.shape[1]), F32)
        carry[...] = jnp.zeros(carry.shape, F32)

    lane = lax.broadcasted_iota(jnp.int32, (UNIT, LANES), 1)
    low = lane < DK_M
    ones_lo = jnp.where((lane >= DH_F) & (lane < DH_F + 3), 1.0, 0.0)
    ones_hi = jnp.where(lane < 3, 1.0, 0.0)
    r8 = lax.broadcasted_iota(jnp.int32, (8, UNIT), 0)

    def groups(z, lo_fill, hi_fill):
        cols = [z[:, LANES * j:LANES * (j + 1)] for j in range(z.shape[1] // LANES)]
        return ([jnp.where(low, c, lo_fill(j)) for j, c in enumerate(cols)]
                + [jnp.where(low, hi_fill(j), c) for j, c in enumerate(cols)])

    def store_groups(ref, rows, gs):
        for g, v in enumerate(gs):
            ref[rows, LANES * g:LANES * (g + 1)] = v.astype(ref.dtype)

    def fold(r):
        return r[0:N_GATES] + r[N_GATES:2 * N_GATES] + r[2 * N_GATES:3 * N_GATES]

    zero = lambda j: 0.0

    def unit(i):
        r0 = i * UNIT
        rows = slice(r0, r0 + UNIT)
        x = x_ref[rows, :]
        xn = _rms(x, n1_ref[...]).astype(BF16)
        yield
        acc = None
        for c in range(0, D_FF, FF_CHUNK):
            part = _swiglu_chunk(xn, c, wg_ref, wu_ref, wd_ref)
            acc = part if acc is None else acc + part
            yield
        h1 = x + 0.5 * acc
        h1_ref[rows, :] = h1
        u = _rms(h1, nm_ref[...]).astype(BF16)
        yield

        zqk = _dot(u, wnat_ref[:, 0:512])
        cbuf[8 + r0:8 + r0 + UNIT, :] = zqk
        yield
        cv = conv_ref[CONV_W - 1:CONV_W, :] * zqk
        for j in range(CONV_W - 1):
            top = 8 + r0 - (CONV_W - 1) + j
            cv = cv + conv_ref[j:j + 1, :] * cbuf[top:top + UNIT, :]
        if i == n_units - 1:
            cbuf[0:8, :] = cbuf[tm:tm + 8, :]
        s = cv * jax.nn.sigmoid(cv)
        store_groups(mq_ref, rows, groups(s[:, 0:256] * (DK_M ** -0.5), zero, zero))
        store_groups(mk_ref, rows, groups(s[:, 256:512], zero, zero))
        yield

        mo_ref[rows, :] = _dot(u, wnat_ref[:, 512:1024]).astype(BF16)
        yield

        zt = lax.dot_general(wt_ref[...], u, NT, preferred_element_type=F32)
        mvt_ref[:, rows] = zt[0:D_MV].astype(BF16)
        fvt_ref[:, rows] = zt[D_MV:D_MV + D_FV].astype(BF16)
        g16 = zt[D_MV + D_FV:D_MV + D_FV + N_GATES] + jnp.tile(bias_ref[...], (1, UNIT // LANES))
        yield
        lf = jnp.minimum(g16, 0.0) - jnp.log1p(jnp.exp(-jnp.abs(g16)))
        xs = jnp.concatenate(_split3(lf), axis=0)
        cum = fold(_dot(xs, ucum_ref[...]))
        bch = fold(_dot(xs, ubd_ref[...]))
        tot = fold(_dot(xs, jnp.ones((UNIT, LANES), BF16)))
        c16 = cum + jnp.tile(carry[...], (1, UNIT // LANES))
        carry[...] = carry[...] + tot
        mg_ref[:, rows] = jnp.where(r8 < NH_M, g16[0:8], bch[0:8])
        ys = jnp.concatenate(_split3(-LOG2E * c16), axis=0)
        place = lax.dot_general(ys, e48_ref[...], TN, preferred_element_type=F32)
        yield
        zfq = _dot(u, wnat_ref[:, 1024:1536]) * (DH_F ** -0.5 * LOG2E)
        store_groups(fq_ref, rows, groups(zfq, lambda j: ones_lo, lambda j: ones_hi))
        yield
        zfk = _dot(u, wnat_ref[:, 1536:2048])
        store_groups(fk_ref, rows, groups(
            zfk,
            lambda j: place[:, LANES * j:LANES * (j + 1)],
            lambda j: place[:, LANES * (j + 4):LANES * (j + 5)]))
        yield

    _skew([unit(i) for i in range(n_units)], lag=N_FF_CHUNKS + 1, early=N_FF_CHUNKS - 2)


def _const_spec(shape):
    nd = len(shape)
    return pl.BlockSpec(shape, lambda *_: (0,) * nd, pipeline_mode=pl.Buffered(1))


def _mix_in(x, n1, wg, wu, wd, nm, wnat, wt, conv, bias16, ucum, ubd, e48):
    B, S, D = x.shape
    n_t = S // TM_IN
    tok = lambda w: pl.BlockSpec((None, TM_IN, w), lambda b, i: (b, i, 0))
    tokt = lambda r: pl.BlockSpec((None, r, TM_IN), lambda b, i: (b, 0, i))
    consts = (n1, wg, wu, wd, nm, wnat, wt, conv, bias16, ucum, ubd, e48)
    out_shape = (
        jax.ShapeDtypeStruct((B, S, D), F32),
        jax.ShapeDtypeStruct((B, S, NH_M * LANES), BF16),
        jax.ShapeDtypeStruct((B, S, NH_M * LANES), BF16),
        jax.ShapeDtypeStruct((B, S, D_MV), BF16),
        jax.ShapeDtypeStruct((B, D_MV, S), BF16),
        jax.ShapeDtypeStruct((B, 8, S), F32),
        jax.ShapeDtypeStruct((B, S, NH_F * LANES), BF16),
        jax.ShapeDtypeStruct((B, S, NH_F * LANES), BF16),
        jax.ShapeDtypeStruct((B, D_FV, S), BF16),
    )
    out_specs = (tok(D), tok(NH_M * LANES), tok(NH_M * LANES), tok(D_MV), tokt(D_MV), tokt(8),
                 tok(NH_F * LANES), tok(NH_F * LANES), tokt(D_FV))
    return pl.pallas_call(
        _mix_in_kernel,
        grid=(B, n_t),
        in_specs=[tok(D)] + [_const_spec(c.shape) for c in consts],
        out_specs=out_specs,
        out_shape=out_shape,
        scratch_shapes=[pltpu.VMEM((TM_IN + 8, 512), F32), pltpu.VMEM((N_GATES, LANES), F32)],
        compiler_params=pltpu.CompilerParams(
            dimension_semantics=("arbitrary", "arbitrary"), vmem_limit_bytes=VMEM_LIMIT),
        name="mix_in",
    )(x, *consts)


def _mlstm_kernel(q_ref, k_ref, vt_ref, mg_ref, mo_ref, gain_ref, o_ref, ct_ref):
    S = q_ref.shape[0]
    L = L_M
    ct_ref[...] = jnp.zeros(ct_ref.shape, F32)
    causal = (lax.broadcasted_iota(jnp.int32, (L, L), 0) <= lax.broadcasted_iota(jnp.int32, (L, L), 1))
    r16 = lax.broadcasted_iota(jnp.int32, (BF16_ROWS, L), 0)
    ones_rows = jnp.where(r16 < 3, 1.0, 0.0).astype(BF16)
    ones16 = jnp.ones((BF16_ROWS, L), BF16)

    def head_chunk(cs, h, ms):
        hs = slice(LANES * h, LANES * (h + 1))
        m = ms[h]
        logi = mg_ref[h:h + 1, pl.ds(cs, L)]
        b = mg_ref[NH_M + h:NH_M + h + 1, pl.ds(cs, L)]
        a = logi - b
        gtot = b[:, L - 1:L]
        m_new = jnp.maximum(gtot + m, gtot + jnp.max(a, axis=1, keepdims=True))
        ms[h] = m_new
        ahi, amid, alo = (p.astype(F32) for p in _split3(a))
        at = jnp.where(r16 == 0, ahi, jnp.where(r16 == 1, amid, jnp.where(r16 == 2, alo, 0.0)))
        acol = lax.dot_general(at.astype(BF16), ones_rows, TN, preferred_element_type=F32)
        q = q_ref[pl.ds(cs, L), hs]
        k = k_ref[pl.ds(cs, L), hs]
        st = lax.dot_general(k, q, NT, preferred_element_type=F32)
        ct = ct_ref[h]
        inter = lax.dot_general(ct.astype(BF16), q, NT, preferred_element_type=F32)
        yield
        e = jnp.where(causal, acol, -jnp.inf)
        mu = jnp.maximum(m, jnp.max(e, axis=0, keepdims=True))
        pt = (jnp.exp(e - mu) * st).astype(BF16)
        vt_aug = jnp.concatenate([vt_ref[hs, pl.ds(cs, L)], ones16], axis=0)
        w = jnp.exp(a + gtot - m_new)
        vtw = (vt_aug.astype(F32) * w).astype(BF16)
        yield
        nd = jnp.exp(m - mu) * inter + _dot(vt_aug, pt)
        ct_ref[h] = jnp.exp(gtot + m - m_new) * ct + _dot(vtw, k)
        yield
        rden = 1.0 / jnp.maximum(jnp.abs(nd[DV_M:DV_M + 1]), jnp.exp(-(b + mu)))
        ht = nd[0:DV_M] * rden
        yt = (ht * lax.rsqrt(jnp.mean(ht * ht, axis=0, keepdims=True) + EPS)
              * jnp.tile(gain_ref[hs, :], (1, L // LANES)))
        og = jax.nn.sigmoid(mo_ref[pl.ds(cs, L), hs].astype(F32))
        o_ref[pl.ds(cs, L), hs] = (yt.T * og).astype(o_ref.dtype)
        yield

    def trip(c, ms):
        ms = list(ms)
        starts = [pl.multiple_of((c * MLSTM_CHUNKS + j) * L, L) for j in range(MLSTM_CHUNKS)]
        _skew([head_chunk(cs, h, ms) for cs in starts for h in range(NH_M)], lag=1, early=0)
        return tuple(ms)

    lax.fori_loop(0, S // (L * MLSTM_CHUNKS), trip, tuple(jnp.zeros((1, 1), F32) for _ in range(NH_M)))


def _mlstm(mq, mk, mvt, mg, mo, gain):
    B, S, _ = mq.shape
    tok = pl.BlockSpec((None, S, NH_M * LANES), lambda b: (b, 0, 0))
    return pl.pallas_call(
        _mlstm_kernel,
        grid=(B,),
        in_specs=[tok, tok,
                  pl.BlockSpec((None, D_MV, S), lambda b: (b, 0, 0)),
                  pl.BlockSpec((None, 8, S), lambda b: (b, 0, 0)),
                  pl.BlockSpec((None, S, D_MV), lambda b: (b, 0, 0)),
                  _const_spec(gain.shape)],
        out_specs=pl.BlockSpec((None, S, D_MV), lambda b: (b, 0, 0)),
        out_shape=jax.ShapeDtypeStruct((B, S, D_MV), BF16),
        scratch_shapes=[pltpu.VMEM((NH_M, DV_M + BF16_ROWS, LANES), F32)],
        compiler_params=pltpu.CompilerParams(
            dimension_semantics=("arbitrary",), vmem_limit_bytes=VMEM_LIMIT),
        name="mlstm",
    )(mq, mk, mvt, mg, mo, gain)


def _fox_kernel(q_ref, k_ref, vt_ref, gain_ref, o_ref, st_a, st_b, mb_a, mb_b, m_ref, acc_ref, bias_ref):
    S = q_ref.shape[0]
    T = T_F
    n_q = S // T
    heads = [slice(LANES * j, LANES * (j + 1)) for j in range(2)]
    bufs = ((st_a, mb_a), (st_b, mb_b))
    n_strips = T // MXU_COLS
    parts = [(j, hc) for j in range(2) for hc in range(n_strips)]
    ones16 = jnp.ones((BF16_ROWS, T), BF16)
    for hc in range(n_strips):
        visible = (lax.broadcasted_iota(jnp.int32, (T, MXU_COLS), 0)
                   <= lax.broadcasted_iota(jnp.int32, (T, MXU_COLS), 1) + hc * MXU_COLS)
        bias_ref[hc] = jnp.where(visible, 0.0, NEG)

    def n_keys(diag, hc):
        return (hc + 1) * MXU_COLS if diag else T

    def logits(c, buf, diag, part):
        qi, kj = c
        st_buf, mb_buf = buf
        j, hc = part
        nk = n_keys(diag, hc)
        qs = pl.multiple_of(qi * T + hc * MXU_COLS, MXU_COLS)
        ks = pl.multiple_of(kj * T, T)
        st = lax.dot_general(k_ref[pl.ds(ks, nk), heads[j]], q_ref[pl.ds(qs, MXU_COLS), heads[j]], NT,
                             preferred_element_type=F32)
        if diag:
            st = st + bias_ref[hc, 0:nk, :]
        st_buf[j, hc, 0:nk, :] = st
        mb_buf[j, hc] = jnp.max(st, axis=0, keepdims=True)

    def update(c, buf, first, part):
        qi, kj = c
        st_buf, mb_buf = buf
        j, hc = part
        nk = n_keys(first, hc)
        ks = pl.multiple_of(kj * T, T)
        m_old = m_ref[qi, j, hc]
        m_new = mb_buf[j, hc] if first else jnp.maximum(m_old, mb_buf[j, hc])
        pt = jnp.exp2(st_buf[j, hc, 0:nk, :] - m_new).astype(BF16)
        vt_aug = jnp.concatenate([vt_ref[DH_F * j:DH_F * (j + 1), pl.ds(ks, nk)], ones16[:, 0:nk]], axis=0)
        pv = _dot(vt_aug, pt)
        acc_ref[qi, j, hc] = pv if first else jnp.exp2(m_old - m_new) * acc_ref[qi, j, hc] + pv
        m_ref[qi, j, hc] = m_new

    def step(cur, nxt, b, diag):
        for part in parts:
            if nxt is not None:
                logits(nxt, bufs[1 - b], diag, part)
            update(cur, bufs[b], diag, part)

    def sweep(n, start, succ, diag):
        if n == 0:
            return
        for part in parts:
            logits(start, bufs[0], diag, part)

        def trip(_, c):
            for i in range(FOX_UNROLL):
                cn = succ(c)
                step(c, cn, i % 2, diag)
                c = cn
            return c

        trips = (n - 1) // FOX_UNROLL
        c = lax.fori_loop(0, trips, trip, start)
        rest = n - trips * FOX_UNROLL
        for i in range(rest):
            cn = succ(c) if i < rest - 1 else None
            step(c, cn, i % 2, diag)
            c = cn

    zero = jnp.int32(0)
    sweep(n_q, (zero, zero), lambda c: (c[0] + 1, c[1] + 1), True)

    def below(c):
        wrap = c[1] + 1 >= c[0]
        return jnp.where(wrap, c[0] + 1, c[0]), jnp.where(wrap, 0, c[1] + 1)

    sweep(n_q * (n_q - 1) // 2, (zero + 1, zero), below, False)

    def finish(qi, _):
        ys = []
        for j in range(2):
            acc = jnp.concatenate([acc_ref[qi, j, hc] for hc in range(n_strips)], axis=1)
            o = acc[0:DH_F] * (1.0 / acc[DH_F:DH_F + 1])
            ys.append(o * lax.rsqrt(jnp.mean(o * o, axis=0, keepdims=True) + EPS)
                      * jnp.tile(gain_ref[DH_F * j:DH_F * (j + 1), :], (1, T // LANES)))
        o_ref[pl.ds(pl.multiple_of(qi * T, T), T), :] = jnp.concatenate(ys, axis=0).T.astype(o_ref.dtype)
        return 0

    lax.fori_loop(0, n_q, finish, 0)


def _fox(fq, fk, fvt, gain):
    B, S, _ = fq.shape
    n_pair = NH_F // 2
    n_strips = T_F // MXU_COLS
    tok = pl.BlockSpec((None, S, 2 * LANES), lambda b, p: (b, 0, p))
    return pl.pallas_call(
        _fox_kernel,
        grid=(B, n_pair),
        in_specs=[tok, tok,
                  pl.BlockSpec((None, 2 * DH_F, S), lambda b, p: (b, p, 0)),
                  pl.BlockSpec((2 * DH_F, LANES), lambda b, p: (p, 0))],
        out_specs=pl.BlockSpec((None, S, 2 * DH_F), lambda b, p: (b, 0, p)),
        out_shape=jax.ShapeDtypeStruct((B, S, D_FV), BF16),
        scratch_shapes=[pltpu.VMEM((2, n_strips, T_F, MXU_COLS), F32), pltpu.VMEM((2, n_strips, T_F, MXU_COLS), F32),
                        pltpu.VMEM((2, n_strips, 1, MXU_COLS), F32), pltpu.VMEM((2, n_strips, 1, MXU_COLS), F32),
                        pltpu.VMEM((S // T_F, 2, n_strips, 1, MXU_COLS), F32),
                        pltpu.VMEM((S // T_F, 2, n_strips, DH_F + BF16_ROWS, MXU_COLS), F32),
                        pltpu.VMEM((n_strips, T_F, MXU_COLS), F32)],
        compiler_params=pltpu.CompilerParams(
            dimension_semantics=("arbitrary", "arbitrary"), vmem_limit_bytes=VMEM_LIMIT),
        name="fox",
    )(fq, fk, fvt, gain)


def _mix_out_kernel(h_ref, ym_ref, yf_ref, p_ref, wom_ref, wof_ref, n2_ref, wg_ref, wu_ref, wd_ref,
                    ng_ref, wpg_ref, wpp_ref, npp_ref, nf_ref, o_ref):
    def unit(i):
        rows = slice(i * UNIT, (i + 1) * UNIT)
        h = h_ref[rows, :] + _dot(ym_ref[rows, :], wom_ref[...]) + _dot(yf_ref[rows, :], wof_ref[...])
        pe = _rms(_dot(p_ref[rows, :].astype(BF16), wpp_ref[...]), npp_ref[...])
        xn = _rms(h, n2_ref[...]).astype(BF16)
        yield
        acc = None
        for c in range(0, D_FF, FF_CHUNK):
            part = _swiglu_chunk(xn, c, wg_ref, wu_ref, wd_ref)
            acc = part if acc is None else acc + part
            yield
        h = h + 0.5 * acc
        xg = _rms(h, ng_ref[...]).astype(BF16)
        yield
        h = h + jax.nn.sigmoid(_dot(xg, wpg_ref[...])) * pe
        o_ref[rows, :] = _rms(h, nf_ref[...])
        yield

    _skew([unit(i) for i in range(h_ref.shape[0] // UNIT)], lag=N_FF_CHUNKS + 1, early=N_FF_CHUNKS - 2)


def _mix_out(h1, ym, yf, p, *consts):
    B, S, D = h1.shape
    tok = lambda w: pl.BlockSpec((None, TM_OUT, w), lambda b, i: (b, i, 0))
    return pl.pallas_call(
        _mix_out_kernel,
        grid=(B, S // TM_OUT),
        in_specs=[tok(D), tok(D_MV), tok(D_FV), tok(p.shape[-1])] + [_const_spec(c.shape) for c in consts],
        out_specs=tok(D),
        out_shape=jax.ShapeDtypeStruct((B, S, D), F32),
        compiler_params=pltpu.CompilerParams(
            dimension_semantics=("arbitrary", "arbitrary"), vmem_limit_bytes=VMEM_LIMIT),
        name="mix_out",
    )(h1, ym, yf, p, *consts)


def _interleave(base, n_heads, width):
    half = n_heads // 2
    idx = []
    for j in range(half):
        idx += list(range(base + j * width, base + (j + 1) * width))
        idx += list(range(base + (j + half) * width, base + (j + half + 1) * width))
    return idx


_O_MQ, _O_MK, _O_MV, _O_MO, _O_MIF = 0, 256, 512, 1024, 1536
_O_FQ, _O_FK, _O_FV, _O_FF = 1544, 2056, 2568, 3080
_QK_PERM = np.array(_interleave(0, NH_M, DK_M) + _interleave(256, NH_M, DK_M))
_NAT_COLS = np.concatenate([
    _QK_PERM, np.arange(_O_MO, _O_MO + D_MV),
    np.array(_interleave(_O_FQ, NH_F, DH_F)), np.array(_interleave(_O_FK, NH_F, DH_F))])
_T_COLS = np.concatenate([np.arange(_O_MV, _O_MV + D_MV), np.arange(_O_FV, _O_FV + D_FV),
                          np.arange(_O_MIF, _O_MIF + 2 * NH_M), np.arange(_O_FF, _O_FF + NH_F)])


def _placement():
    e = np.zeros((3 * N_GATES, NH_F * LANES), np.float32)
    for j in range(3):
        for h in range(NH_F):
            e[N_GATES * j + 2 * NH_M + h, LANES * h + (DH_F + j if h < NH_F // 2 else j)] = 1.0
    return e


def _tri(n, block):
    s = np.arange(n)[:, None]
    t = np.arange(n)[None, :]
    return ((s <= t) & (s // block == t // block)).astype(np.float32)


def _take_cols(w, cols):
    cuts = [0] + [i for i in range(1, len(cols)) if cols[i] != cols[i - 1] + 1] + [len(cols)]
    return jnp.concatenate([w[:, int(cols[a]):int(cols[b - 1]) + 1] for a, b in zip(cuts[:-1], cuts[1:])], axis=1)


def kernel(x, p, ffn1_norm, ffn1_w_gate, ffn1_w_up, ffn1_w_down, mix_norm, w_in, conv_qk, b_mlstm_gates, b_fox_f, mlstm_out_norm, fox_out_norm, w_out, ffn2_norm, ffn2_w_gate, ffn2_w_up, ffn2_w_down, ple_gate_norm, w_ple_gate, w_ple_proj, ple_proj_norm, final_norm):
    depth = w_in.shape[0]
    assert depth == 1, "the final RMSNorm is fused into the (single) layer's last kernel"
    assert x.shape[1] % TM_OUT == 0 and x.shape[2] == D_MODEL
    assert TM_IN % UNIT == 0 and TM_OUT % UNIT == 0 and UNIT % L_M == 0 and FOX_UNROLL % 2 == 0
    row = lambda v: v.reshape(1, -1).astype(F32)
    col = lambda v: jnp.broadcast_to(v.astype(F32)[:, None], (v.shape[0], LANES))
    ucum = jnp.asarray(_tri(UNIT, UNIT), BF16)
    ubd = jnp.asarray(_tri(UNIT, L_M), BF16)
    e48 = jnp.asarray(_placement(), BF16)
    h = x
    for i in range(depth):
        wnat = _take_cols(w_in[i], _NAT_COLS).astype(BF16)
        wt = _take_cols(w_in[i], _T_COLS).T.astype(BF16)
        conv = _take_cols(conv_qk[i], _QK_PERM).astype(F32)
        bias16 = col(jnp.concatenate([b_mlstm_gates[i], b_fox_f[i]]))
        h1, mq, mk, mo, mvt, mg, fq, fk, fvt = _mix_in(
            h, row(ffn1_norm[i]), ffn1_w_gate[i].astype(BF16), ffn1_w_up[i].astype(BF16),
            ffn1_w_down[i].astype(BF16), row(mix_norm[i]), wnat, wt, conv, bias16, ucum, ubd, e48)
        ym = _mlstm(mq, mk, mvt, mg, mo, col(mlstm_out_norm[i]))
        yf = _fox(fq, fk, fvt, col(fox_out_norm[i]))
        wo = w_out[i].astype(BF16)
        h = _mix_out(
            h1, ym, yf, p[i], wo[:D_MV], wo[D_MV:], row(ffn2_norm[i]), ffn2_w_gate[i].astype(BF16),
            ffn2_w_up[i].astype(BF16), ffn2_w_down[i].astype(BF16), row(ple_gate_norm[i]),
            w_ple_gate[i].astype(BF16), w_ple_proj[i].astype(BF16), row(ple_proj_norm[i]),
            row(final_norm))
    return h
```

```python
import numpy as np
import jax
import jax.numpy as jnp
from jax import lax
from jax.experimental import pallas as pl
from jax.experimental.pallas import tpu as pltpu

F32 = jnp.float32
BF16 = jnp.bfloat16

EPS = 1e-6
D_MODEL = 1024
D_FF = 2816
NH_M = 4
DV_M = 128
DK_M = 64
CONV_W = 4
NH_F = 8
DH_F = 64
D_MV = NH_M * DV_M
D_FV = NH_F * DH_F
N_GATES = 2 * NH_M + NH_F

LANES = 128
BF16_ROWS = 16
MXU_COLS = 256
TM_IN = 512
TM_OUT = 1024
UNIT = 512
FF_CHUNK = 256
N_FF_CHUNKS = D_FF // FF_CHUNK
L_M = 256
MLSTM_CHUNKS = 4
T_F = 512
LOG2E = 1.4426950408889634
FOX_UNROLL = 8
FOX_FINISH_UNROLL = 4
NEG = -1e30
VMEM_LIMIT = 60 * 1024 * 1024

NT = (((1,), (1,)), ((), ()))
TN = (((0,), (0,)), ((), ()))


def _rms(x, g):
    return x * lax.rsqrt(jnp.mean(x * x, axis=-1, keepdims=True) + EPS) * g


def _dot(a, b):
    return jnp.dot(a, b, preferred_element_type=F32)


def _split3(x):
    hi = x.astype(BF16)
    r = x - hi.astype(F32)
    mid = r.astype(BF16)
    lo = (r - mid.astype(F32)).astype(BF16)
    return hi, mid, lo


def _swiglu_chunk(xn, c, wg_ref, wu_ref, wd_ref):
    a = _dot(xn, wg_ref[:, c:c + FF_CHUNK])
    b = _dot(xn, wu_ref[:, c:c + FF_CHUNK])
    return _dot((a * jax.nn.sigmoid(a) * b).astype(BF16), wd_ref[c:c + FF_CHUNK, :])


def _skew(units, lag, early):
    done = [0] * len(units)
    tick = 0
    while any(d >= 0 for d in done):
        for i, u in enumerate(units):
            due = max(0, i * lag - early) if done[i] == 0 else i * lag + done[i]
            if done[i] >= 0 and tick >= due:
                try:
                    next(u)
                    done[i] += 1
                except StopIteration:
                    done[i] = -1
        tick += 1


def _mix_in_kernel(x_ref, n1_ref, wg_ref, wu_ref, wd_ref, nm_ref, wnat_ref, wt_ref, conv_ref,
                   bias_ref, ucum_ref, ubd_ref, e48_ref,
                   h1_ref, mq_ref, mk_ref, mo_ref, mvt_ref, mg_ref, fq_ref, fk_ref, fvt_ref,
                   cbuf, carry):
    tm = x_ref.shape[0]
    n_units = tm // UNIT

    @pl.when(pl.program_id(1) == 0)
    def _():
        cbuf[0:8, :] = jnp.zeros((8, cbuf.shape[1]), F32)
        carry[...] = jnp.zeros(carry.shape, F32)

    lane = lax.broadcasted_iota(jnp.int32, (UNIT, LANES), 1)
    low = lane < DK_M
    ones_lo = jnp.where((lane >= DH_F) & (lane < DH_F + 3), 1.0, 0.0)
    ones_hi = jnp.where(lane < 3, 1.0, 0.0)
    r8 = lax.broadcasted_iota(jnp.int32, (8, UNIT), 0)

    def groups(z, lo_fill, hi_fill):
        cols = [z[:, LANES * j:LANES * (j + 1)] for j in range(z.shape[1] // LANES)]
        return ([jnp.where(low, c, lo_fill(j)) for j, c in enumerate(cols)]
                + [jnp.where(low, hi_fill(j), c) for j, c in enumerate(cols)])

    def store_groups(ref, rows, gs):
        for g, v in enumerate(gs):
            ref[rows, LANES * g:LANES * (g + 1)] = v.astype(ref.dtype)

    def fold(r):
        return r[0:N_GATES] + r[N_GATES:2 * N_GATES] + r[2 * N_GATES:3 * N_GATES]

    zero = lambda j: 0.0

    def unit(i):
        r0 = i * UNIT
        rows = slice(r0, r0 + UNIT)
        x = x_ref[rows, :]
        xn = _rms(x, n1_ref[...]).astype(BF16)
        yield
        acc = None
        for c in range(0, D_FF, FF_CHUNK):
            part = _swiglu_chunk(xn, c, wg_ref, wu_ref, wd_ref)
            acc = part if acc is None else acc + part
            yield
        h1 = x + 0.5 * acc
        h1_ref[rows, :] = h1
        u = _rms(h1, nm_ref[...]).astype(BF16)
        yield

        zqk = _dot(u, wnat_ref[:, 0:512])
        cbuf[8 + r0:8 + r0 + UNIT, :] = zqk
        yield
        cv = conv_ref[CONV_W - 1:CONV_W, :] * zqk
        for j in range(CONV_W - 1):
            top = 8 + r0 - (CONV_W - 1) + j
            cv = cv + conv_ref[j:j + 1, :] * cbuf[top:top + UNIT, :]
        if i == n_units - 1:
            cbuf[0:8, :] = cbuf[tm:tm + 8, :]
        s = cv * jax.nn.sigmoid(cv)
        store_groups(mq_ref, rows, groups(s[:, 0:256] * (DK_M ** -0.5), zero, zero))
        store_groups(mk_ref, rows, groups(s[:, 256:512], zero, zero))
        yield

        mo_ref[rows, :] = _dot(u, wnat_ref[:, 512:1024]).astype(BF16)
        yield

        zt = lax.dot_general(wt_ref[...], u, NT, preferred_element_type=F32)
        mvt_ref[:, rows] = zt[0:D_MV].astype(BF16)
        fvt_ref[:, rows] = zt[D_MV:D_MV + D_FV].astype(BF16)
        g16 = zt[D_MV + D_FV:D_MV + D_FV + N_GATES] + jnp.tile(bias_ref[...], (1, UNIT // LANES))
        yield
        lf = jnp.minimum(g16, 0.0) - jnp.log1p(jnp.exp(-jnp.abs(g16)))
        xs = jnp.concatenate(_split3(lf), axis=0)
        cum = fold(_dot(xs, ucum_ref[...]))
        bch = fold(_dot(xs, ubd_ref[...]))
        tot = fold(_dot(xs, jnp.ones((UNIT, LANES), BF16)))
        c16 = cum + jnp.tile(carry[...], (1, UNIT // LANES))
        carry[...] = carry[...] + tot
        mg_ref[:, rows] = jnp.where(r8 < NH_M, g16[0:8], bch[0:8])
        ys = jnp.concatenate(_split3(-LOG2E * c16), axis=0)
        place = lax.dot_general(ys, e48_ref[...], TN, preferred_element_type=F32)
        yield
        zfq = _dot(u, wnat_ref[:, 1024:1536]) * (DH_F ** -0.5 * LOG2E)
        store_groups(fq_ref, rows, groups(zfq, lambda j: ones_lo, lambda j: ones_hi))
        yield
        zfk = _dot(u, wnat_ref[:, 1536:2048])
        store_groups(fk_ref, rows, groups(
            zfk,
            lambda j: place[:, LANES * j:LANES * (j + 1)],
            lambda j: place[:, LANES * (j + 4):LANES * (j + 5)]))
        yield

    _skew([unit(i) for i in range(n_units)], lag=N_FF_CHUNKS + 1, early=N_FF_CHUNKS - 2)


def _const_spec(shape):
    nd = len(shape)
    return pl.BlockSpec(shape, lambda *_: (0,) * nd, pipeline_mode=pl.Buffered(1))


def _mix_in(x, n1, wg, wu, wd, nm, wnat, wt, conv, bias16, ucum, ubd, e48):
    B, S, D = x.shape
    n_t = S // TM_IN
    tok = lambda w: pl.BlockSpec((None, TM_IN, w), lambda b, i: (b, i, 0))
    tokt = lambda r: pl.BlockSpec((None, r, TM_IN), lambda b, i: (b, 0, i))
    consts = (n1, wg, wu, wd, nm, wnat, wt, conv, bias16, ucum, ubd, e48)
    out_shape = (
        jax.ShapeDtypeStruct((B, S, D), F32),
        jax.ShapeDtypeStruct((B, S, NH_M * LANES), BF16),
        jax.ShapeDtypeStruct((B, S, NH_M * LANES), BF16),
        jax.ShapeDtypeStruct((B, S, D_MV), BF16),
        jax.ShapeDtypeStruct((B, D_MV, S), BF16),
        jax.ShapeDtypeStruct((B, 8, S), F32),
        jax.ShapeDtypeStruct((B, S, NH_F * LANES), BF16),
        jax.ShapeDtypeStruct((B, S, NH_F * LANES), BF16),
        jax.ShapeDtypeStruct((B, D_FV, S), BF16),
    )
    out_specs = (tok(D), tok(NH_M * LANES), tok(NH_M * LANES), tok(D_MV), tokt(D_MV), tokt(8),
                 tok(NH_F * LANES), tok(NH_F * LANES), tokt(D_FV))
    return pl.pallas_call(
        _mix_in_kernel,
        grid=(B, n_t),
        in_specs=[tok(D)] + [_const_spec(c.shape) for c in consts],
        out_specs=out_specs,
        out_shape=out_shape,
        scratch_shapes=[pltpu.VMEM((TM_IN + 8, 512), F32), pltpu.VMEM((N_GATES, LANES), F32)],
        compiler_params=pltpu.CompilerParams(
            dimension_semantics=("arbitrary", "arbitrary"), vmem_limit_bytes=VMEM_LIMIT),
        name="mix_in",
    )(x, *consts)


def _mlstm_kernel(q_ref, k_ref, vt_ref, mg_ref, mo_ref, gain_ref, o_ref, ct_ref):
    S = q_ref.shape[0]
    L = L_M
    ct_ref[...] = jnp.zeros(ct_ref.shape, F32)
    causal = (lax.broadcasted_iota(jnp.int32, (L, L), 0) <= lax.broadcasted_iota(jnp.int32, (L, L), 1))
    r16 = lax.broadcasted_iota(jnp.int32, (BF16_ROWS, L), 0)
    ones_rows = jnp.where(r16 < 3, 1.0, 0.0).astype(BF16)
    ones16 = jnp.ones((BF16_ROWS, L), BF16)

    def head_chunk(cs, h, ms):
        hs = slice(LANES * h, LANES * (h + 1))
        m = ms[h]
        logi = mg_ref[h:h + 1, pl.ds(cs, L)]
        b = mg_ref[NH_M + h:NH_M + h + 1, pl.ds(cs, L)]
        a = logi - b
        gtot = b[:, L - 1:L]
        m_new = jnp.maximum(gtot + m, gtot + jnp.max(a, axis=1, keepdims=True))
        ms[h] = m_new
        ahi, amid, alo = (p.astype(F32) for p in _split3(a))
        at = jnp.where(r16 == 0, ahi, jnp.where(r16 == 1, amid, jnp.where(r16 == 2, alo, 0.0)))
        acol = lax.dot_general(at.astype(BF16), ones_rows, TN, preferred_element_type=F32)
        q = q_ref[pl.ds(cs, L), hs]
        k = k_ref[pl.ds(cs, L), hs]
        st = lax.dot_general(k, q, NT, preferred_element_type=F32)
        ct = ct_ref[h]
        inter = lax.dot_general(ct.astype(BF16), q, NT, preferred_element_type=F32)
        yield
        e = jnp.where(causal, acol, -jnp.inf)
        mu = jnp.maximum(m, jnp.max(e, axis=0, keepdims=True))
        pt = (jnp.exp(e - mu) * st).astype(BF16)
        vt_aug = jnp.concatenate([vt_ref[hs, pl.ds(cs, L)], ones16], axis=0)
        w = jnp.exp(a + gtot - m_new)
        vtw = (vt_aug.astype(F32) * w).astype(BF16)
        yield
        nd = jnp.exp(m - mu) * inter + _dot(vt_aug, pt)
        ct_ref[h] = jnp.exp(gtot + m - m_new) * ct + _dot(vtw, k)
        yield
        rden = 1.0 / jnp.maximum(jnp.abs(nd[DV_M:DV_M + 1]), jnp.exp(-(b + mu)))
        ht = nd[0:DV_M] * rden
        yt = (ht * lax.rsqrt(jnp.mean(ht * ht, axis=0, keepdims=True) + EPS)
              * jnp.tile(gain_ref[hs, :], (1, L // LANES)))
        og = jax.nn.sigmoid(mo_ref[pl.ds(cs, L), hs].astype(F32))
        o_ref[pl.ds(cs, L), hs] = (yt.T * og).astype(o_ref.dtype)
        yield

    def trip(c, ms):
        ms = list(ms)
        starts = [pl.multiple_of((c * MLSTM_CHUNKS + j) * L, L) for j in range(MLSTM_CHUNKS)]
        _skew([head_chunk(cs, h, ms) for cs in starts for h in range(NH_M)], lag=1, early=0)
        return tuple(ms)

    lax.fori_loop(0, S // (L * MLSTM_CHUNKS), trip, tuple(jnp.zeros((1, 1), F32) for _ in range(NH_M)))


def _mlstm(mq, mk, mvt, mg, mo, gain):
    B, S, _ = mq.shape
    tok = pl.BlockSpec((None, S, NH_M * LANES), lambda b: (b, 0, 0))
    return pl.pallas_call(
        _mlstm_kernel,
        grid=(B,),
        in_specs=[tok, tok,
                  pl.BlockSpec((None, D_MV, S), lambda b: (b, 0, 0)),
                  pl.BlockSpec((None, 8, S), lambda b: (b, 0, 0)),
                  pl.BlockSpec((None, S, D_MV), lambda b: (b, 0, 0)),
                  _const_spec(gain.shape)],
        out_specs=pl.BlockSpec((None, S, D_MV), lambda b: (b, 0, 0)),
        out_shape=jax.ShapeDtypeStruct((B, S, D_MV), BF16),
        scratch_shapes=[pltpu.VMEM((NH_M, DV_M + BF16_ROWS, LANES), F32)],
        compiler_params=pltpu.CompilerParams(
            dimension_semantics=("arbitrary",), vmem_limit_bytes=VMEM_LIMIT),
        name="mlstm",
    )(mq, mk, mvt, mg, mo, gain)


def _fox_kernel(q_ref, k_ref, vt_ref, gain_ref, o_ref, st_a, st_b, mb_a, mb_b, m_ref, acc_ref, bias_ref):
    S = q_ref.shape[0]
    T = T_F
    n_q = S // T
    heads = [slice(LANES * j, LANES * (j + 1)) for j in range(2)]
    bufs = ((st_a, mb_a), (st_b, mb_b))
    n_strips = T // MXU_COLS
    parts = [(j, hc) for j in range(2) for hc in range(n_strips)]
    ones16 = jnp.ones((BF16_ROWS, T), BF16)
    for hc in range(n_strips):
        visible = (lax.broadcasted_iota(jnp.int32, (T, MXU_COLS), 0)
                   <= lax.broadcasted_iota(jnp.int32, (T, MXU_COLS), 1) + hc * MXU_COLS)
        bias_ref[hc] = jnp.where(visible, 0.0, NEG)

    def n_keys(diag, hc):
        return (hc + 1) * MXU_COLS if diag else T

    def logits(c, buf, diag, part):
        qi, kj = c
        st_buf, mb_buf = buf
        j, hc = part
        nk = n_keys(diag, hc)
        qs = pl.multiple_of(qi * T + hc * MXU_COLS, MXU_COLS)
        ks = pl.multiple_of(kj * T, T)
        st = lax.dot_general(k_ref[pl.ds(ks, nk), heads[j]], q_ref[pl.ds(qs, MXU_COLS), heads[j]], NT,
                             preferred_element_type=F32)
        if diag:
            st = st + bias_ref[hc, 0:nk, :]
        st_buf[j, hc, 0:nk, :] = st
        mb_buf[j, hc] = jnp.max(st, axis=0, keepdims=True)

    def update(c, buf, first, part):
        qi, kj = c
        st_buf, mb_buf = buf
        j, hc = part
        nk = n_keys(first, hc)
        ks = pl.multiple_of(kj * T, T)
        m_old = m_ref[qi, j, hc]
        m_new = mb_buf[j, hc] if first else jnp.maximum(m_old, mb_buf[j, hc])
        pt = jnp.exp2(st_buf[j, hc, 0:nk, :] - m_new).astype(BF16)
        vt_aug = jnp.concatenate([vt_ref[DH_F * j:DH_F * (j + 1), pl.ds(ks, nk)], ones16[:, 0:nk]], axis=0)
        pv = _dot(vt_aug, pt)
        acc_ref[qi, j, hc] = pv if first else jnp.exp2(m_old - m_new) * acc_ref[qi, j, hc] + pv
        m_ref[qi, j, hc] = m_new

    def step(cur, nxt, b, diag):
        for part in parts:
            if nxt is not None:
                logits(nxt, bufs[1 - b], diag, part)
            update(cur, bufs[b], diag, part)

    def sweep(n, start, succ, diag):
        if n == 0:
            return
        for part in parts:
            logits(start, bufs[0], diag, part)

        def trip(_, c):
            for i in range(FOX_UNROLL):
                cn = succ(c)
                step(c, cn, i % 2, diag)
                c = cn
            return c

        trips = (n - 1) // FOX_UNROLL
        c = lax.fori_loop(0, trips, trip, start)
        rest = n - trips * FOX_UNROLL
        for i in range(rest):
            cn = succ(c) if i < rest - 1 else None
            step(c, cn, i % 2, diag)
            c = cn

    zero = jnp.int32(0)
    sweep(n_q, (zero, zero), lambda c: (c[0] + 1, c[1] + 1), True)

    def below(c):
        wrap = c[1] + 1 >= c[0]
        return jnp.where(wrap, c[0] + 1, c[0]), jnp.where(wrap, 0, c[1] + 1)

    sweep(n_q * (n_q - 1) // 2, (zero + 1, zero), below, False)

    def finish(qi, _):
        ys = []
        for j in range(2):
            acc = jnp.concatenate([acc_ref[qi, j, hc] for hc in range(n_strips)], axis=1)
            o = acc[0:DH_F] * (1.0 / acc[DH_F:DH_F + 1])
            ys.append(o * lax.rsqrt(jnp.mean(o * o, axis=0, keepdims=True) + EPS)
                      * jnp.tile(gain_ref[DH_F * j:DH_F * (j + 1), :], (1, T // LANES)))
        o_ref[pl.ds(pl.multiple_of(qi * T, T), T), :] = jnp.concatenate(ys, axis=0).T.astype(o_ref.dtype)
        return 0

    lax.fori_loop(0, n_q, finish, 0, unroll=FOX_FINISH_UNROLL)


def _fox(fq, fk, fvt, gain):
    B, S, _ = fq.shape
    n_pair = NH_F // 2
    n_strips = T_F // MXU_COLS
    tok = pl.BlockSpec((None, S, 2 * LANES), lambda b, p: (b, 0, p))
    return pl.pallas_call(
        _fox_kernel,
        grid=(B, n_pair),
        in_specs=[tok, tok,
                  pl.BlockSpec((None, 2 * DH_F, S), lambda b, p: (b, p, 0)),
                  pl.BlockSpec((2 * DH_F, LANES), lambda b, p: (p, 0))],
        out_specs=pl.BlockSpec((None, S, 2 * DH_F), lambda b, p: (b, 0, p)),
        out_shape=jax.ShapeDtypeStruct((B, S, D_FV), BF16),
        scratch_shapes=[pltpu.VMEM((2, n_strips, T_F, MXU_COLS), F32), pltpu.VMEM((2, n_strips, T_F, MXU_COLS), F32),
                        pltpu.VMEM((2, n_strips, 1, MXU_COLS), F32), pltpu.VMEM((2, n_strips, 1, MXU_COLS), F32),
                        pltpu.VMEM((S // T_F, 2, n_strips, 1, MXU_COLS), F32),
                        pltpu.VMEM((S // T_F, 2, n_strips, DH_F + BF16_ROWS, MXU_COLS), F32),
                        pltpu.VMEM((n_strips, T_F, MXU_COLS), F32)],
        compiler_params=pltpu.CompilerParams(
            dimension_semantics=("arbitrary", "arbitrary"), vmem_limit_bytes=VMEM_LIMIT),
        name="fox",
    )(fq, fk, fvt, gain)


def _mix_out_kernel(h_ref, ym_ref, yf_ref, p_ref, wom_ref, wof_ref, n2_ref, wg_ref, wu_ref, wd_ref,
                    ng_ref, wpg_ref, wpp_ref, npp_ref, nf_ref, o_ref):
    def unit(i):
        rows = slice(i * UNIT, (i + 1) * UNIT)
        h = h_ref[rows, :] + _dot(ym_ref[rows, :], wom_ref[...]) + _dot(yf_ref[rows, :], wof_ref[...])
        pe = _rms(_dot(p_ref[rows, :].astype(BF16), wpp_ref[...]), npp_ref[...])
        xn = _rms(h, n2_ref[...]).astype(BF16)
        yield
        acc = None
        for c in range(0, D_FF, FF_CHUNK):
            part = _swiglu_chunk(xn, c, wg_ref, wu_ref, wd_ref)
            acc = part if acc is None else acc + part
            yield
        h = h + 0.5 * acc
        xg = _rms(h, ng_ref[...]).astype(BF16)
        yield
        h = h + jax.nn.sigmoid(_dot(xg, wpg_ref[...])) * pe
        o_ref[rows, :] = _rms(h, nf_ref[...])
        yield

    _skew([unit(i) for i in range(h_ref.shape[0] // UNIT)], lag=N_FF_CHUNKS + 1, early=N_FF_CHUNKS - 2)


def _mix_out(h1, ym, yf, p, *consts):
    B, S, D = h1.shape
    tok = lambda w: pl.BlockSpec((None, TM_OUT, w), lambda b, i: (b, i, 0))
    return pl.pallas_call(
        _mix_out_kernel,
        grid=(B, S // TM_OUT),
        in_specs=[tok(D), tok(D_MV), tok(D_FV), tok(p.shape[-1])] + [_const_spec(c.shape) for c in consts],
        out_specs=tok(D),
        out_shape=jax.ShapeDtypeStruct((B, S, D), F32),
        compiler_params=pltpu.CompilerParams(
            dimension_semantics=("arbitrary", "arbitrary"), vmem_limit_bytes=VMEM_LIMIT),
        name="mix_out",
    )(h1, ym, yf, p, *consts)


def _interleave(base, n_heads, width):
    half = n_heads // 2
    idx = []
    for j in range(half):
        idx += list(range(base + j * width, base + (j + 1) * width))
        idx += list(range(base + (j + half) * width, base + (j + half + 1) * width))
    return idx


_O_MQ, _O_MK, _O_MV, _O_MO, _O_MIF = 0, 256, 512, 1024, 1536
_O_FQ, _O_FK, _O_FV, _O_FF = 1544, 2056, 2568, 3080
_QK_PERM = np.array(_interleave(0, NH_M, DK_M) + _interleave(256, NH_M, DK_M))
_NAT_COLS = np.concatenate([
    _QK_PERM, np.arange(_O_MO, _O_MO + D_MV),
    np.array(_interleave(_O_FQ, NH_F, DH_F)), np.array(_interleave(_O_FK, NH_F, DH_F))])
_T_COLS = np.concatenate([np.arange(_O_MV, _O_MV + D_MV), np.arange(_O_FV, _O_FV + D_FV),
                          np.arange(_O_MIF, _O_MIF + 2 * NH_M), np.arange(_O_FF, _O_FF + NH_F)])


def _placement():
    e = np.zeros((3 * N_GATES, NH_F * LANES), np.float32)
    for j in range(3):
        for h in range(NH_F):
            e[N_GATES * j + 2 * NH_M + h, LANES * h + (DH_F + j if h < NH_F // 2 else j)] = 1.0
    return e


def _tri(n, block):
    s = np.arange(n)[:, None]
    t = np.arange(n)[None, :]
    return ((s <= t) & (s // block == t // block)).astype(np.float32)


def _take_cols(w, cols):
    cuts = [0] + [i for i in range(1, len(cols)) if cols[i] != cols[i - 1] + 1] + [len(cols)]
    return jnp.concatenate([w[:, int(cols[a]):int(cols[b - 1]) + 1] for a, b in zip(cuts[:-1], cuts[1:])], axis=1)


def kernel(x, p, ffn1_norm, ffn1_w_gate, ffn1_w_up, ffn1_w_down, mix_norm, w_in, conv_qk, b_mlstm_gates, b_fox_f, mlstm_out_norm, fox_out_norm, w_out, ffn2_norm, ffn2_w_gate, ffn2_w_up, ffn2_w_down, ple_gate_norm, w_ple_gate, w_ple_proj, ple_proj_norm, final_norm):
    depth = w_in.shape[0]
    assert depth == 1, "the final RMSNorm is fused into the (single) layer's last kernel"
    assert x.shape[1] % TM_OUT == 0 and x.shape[1] % (L_M * MLSTM_CHUNKS) == 0 and x.shape[2] == D_MODEL
    assert TM_IN % UNIT == 0 and TM_OUT % UNIT == 0 and UNIT % L_M == 0 and FOX_UNROLL % 2 == 0
    row = lambda v: v.reshape(1, -1).astype(F32)
    col = lambda v: jnp.broadcast_to(v.astype(F32)[:, None], (v.shape[0], LANES))
    ucum = jnp.asarray(_tri(UNIT, UNIT), BF16)
    ubd = jnp.asarray(_tri(UNIT, L_M), BF16)
    e48 = jnp.asarray(_placement(), BF16)
    h = x
    for i in range(depth):
        wnat = _take_cols(w_in[i], _NAT_COLS).astype(BF16)
        wt = _take_cols(w_in[i], _T_COLS).T.astype(BF16)
        conv = _take_cols(conv_qk[i], _QK_PERM).astype(F32)
        bias16 = col(jnp.concatenate([b_mlstm_gates[i], b_fox_f[i]]))
        h1, mq, mk, mo, mvt, mg, fq, fk, fvt = _mix_in(
            h, row(ffn1_norm[i]), ffn1_w_gate[i].astype(BF16), ffn1_w_up[i].astype(BF16),
            ffn1_w_down[i].astype(BF16), row(mix_norm[i]), wnat, wt, conv, bias16, ucum, ubd, e48)
        ym = _mlstm(mq, mk, mvt, mg, mo, col(mlstm_out_norm[i]))
        yf = _fox(fq, fk, fvt, col(fox_out_norm[i]))
        wo = w_out[i].astype(BF16)
        h = _mix_out(
            h1, ym, yf, p[i], wo[:D_MV], wo[D_MV:], row(ffn2_norm[i]), ffn2_w_gate[i].astype(BF16),
            ffn2_w_up[i].astype(BF16), ffn2_w_down[i].astype(BF16), row(ple_gate_norm[i]),
            w_ple_gate[i].astype(BF16), w_ple_proj[i].astype(BF16), row(ple_proj_norm[i]),
            row(final_norm))
    return h
```

```python
import numpy as np
import jax
import jax.numpy as jnp
from jax import lax
from jax.experimental import pallas as pl
from jax.experimental.pallas import tpu as pltpu

F32 = jnp.float32
BF16 = jnp.bfloat16

EPS = 1e-6
D_MODEL = 1024
D_FF = 2816
NH_M = 4
DV_M = 128
DK_M = 64
CONV_W = 4
NH_F = 8
DH_F = 64
D_MV = NH_M * DV_M
D_FV = NH_F * DH_F
N_GATES = 2 * NH_M + NH_F

LANES = 128
BF16_ROWS = 16
MXU_COLS = 256
TM_IN = 512
TM_OUT = 1024
UNIT = 512
FF_CHUNK = 256
N_FF_CHUNKS = D_FF // FF_CHUNK
L_M = 256
MLSTM_CHUNKS = 4
T_F = 512
LOG2E = 1.4426950408889634
FOX_UNROLL = 8
FOX_FINISH_UNROLL = 4
NEG = -1e30
VMEM_LIMIT = 60 * 1024 * 1024

NT = (((1,), (1,)), ((), ()))
TN = (((0,), (0,)), ((), ()))


def _rms(x, g):
    return x * lax.rsqrt(jnp.mean(x * x, axis=-1, keepdims=True) + EPS) * g


def _dot(a, b):
    return jnp.dot(a, b, preferred_element_type=F32)


def _split3(x):
    hi = x.astype(BF16)
    r = x - hi.astype(F32)
    mid = r.astype(BF16)
    lo = (r - mid.astype(F32)).astype(BF16)
    return hi, mid, lo


def _swiglu_chunk(xn, c, wg_ref, wu_ref, wd_ref):
    a = _dot(xn, wg_ref[:, c:c + FF_CHUNK])
    b = _dot(xn, wu_ref[:, c:c + FF_CHUNK])
    return _dot((a * jax.nn.sigmoid(a) * b).astype(BF16), wd_ref[c:c + FF_CHUNK, :])


def _skew(units, lag, early):
    done = [0] * len(units)
    tick = 0
    while any(d >= 0 for d in done):
        for i, u in enumerate(units):
            due = max(0, i * lag - early) if done[i] == 0 else i * lag + done[i]
            if done[i] >= 0 and tick >= due:
                try:
                    next(u)
                    done[i] += 1
                except StopIteration:
                    done[i] = -1
        tick += 1


def _mix_in_kernel(x_ref, n1_ref, wg_ref, wu_ref, wd_ref, nm_ref, wnat_ref, wt_ref, conv_ref,
                   bias_ref, ucum_ref, ubd_ref, e48_ref,
                   h1_ref, mq_ref, mk_ref, mo_ref, mvt_ref, mg_ref, fq_ref, fk_ref, fvt_ref,
                   cbuf, carry):
    tm = x_ref.shape[0]
    n_units = tm // UNIT

    @pl.when(pl.program_id(1) == 0)
    def _():
        cbuf[0:8, :] = jnp.zeros((8, cbuf.shape[1]), F32)
        carry[...] = jnp.zeros(carry.shape, F32)

    lane = lax.broadcasted_iota(jnp.int32, (UNIT, LANES), 1)
    low = lane < DK_M
    ones_lo = jnp.where((lane >= DH_F) & (lane < DH_F + 3), 1.0, 0.0)
    ones_hi = jnp.where(lane < 3, 1.0, 0.0)
    r8 = lax.broadcasted_iota(jnp.int32, (8, UNIT), 0)

    def groups(z, fill):
        out = []
        for j in range(z.shape[1] // LANES):
            c = z[:, LANES * j:LANES * (j + 1)]
            out += [jnp.where(low, c, fill(2 * j)), jnp.where(low, fill(2 * j + 1), c)]
        return out

    def store_groups(ref, rows, gs):
        for g, v in enumerate(gs):
            ref[rows, LANES * g:LANES * (g + 1)] = v.astype(ref.dtype)

    def fold(r):
        return r[0:N_GATES] + r[N_GATES:2 * N_GATES] + r[2 * N_GATES:3 * N_GATES]

    zero = lambda j: 0.0

    def unit(i):
        r0 = i * UNIT
        rows = slice(r0, r0 + UNIT)
        x = x_ref[rows, :]
        xn = _rms(x, n1_ref[...]).astype(BF16)
        yield
        acc = None
        for c in range(0, D_FF, FF_CHUNK):
            part = _swiglu_chunk(xn, c, wg_ref, wu_ref, wd_ref)
            acc = part if acc is None else acc + part
            yield
        h1 = x + 0.5 * acc
        h1_ref[rows, :] = h1
        u = _rms(h1, nm_ref[...]).astype(BF16)
        yield

        zqk = _dot(u, wnat_ref[:, 0:512])
        cbuf[8 + r0:8 + r0 + UNIT, :] = zqk
        yield
        cv = conv_ref[CONV_W - 1:CONV_W, :] * zqk
        for j in range(CONV_W - 1):
            top = 8 + r0 - (CONV_W - 1) + j
            cv = cv + conv_ref[j:j + 1, :] * cbuf[top:top + UNIT, :]
        if i == n_units - 1:
            cbuf[0:8, :] = cbuf[tm:tm + 8, :]
        s = cv * jax.nn.sigmoid(cv)
        store_groups(mq_ref, rows, groups(s[:, 0:256] * (DK_M ** -0.5), zero))
        store_groups(mk_ref, rows, groups(s[:, 256:512], zero))
        yield

        mo_ref[rows, :] = _dot(u, wnat_ref[:, 512:1024]).astype(BF16)
        yield

        zt = lax.dot_general(wt_ref[...], u, NT, preferred_element_type=F32)
        mvt_ref[:, rows] = zt[0:D_MV].astype(BF16)
        fvt_ref[:, rows] = zt[D_MV:D_MV + D_FV].astype(BF16)
        g16 = zt[D_MV + D_FV:D_MV + D_FV + N_GATES] + jnp.tile(bias_ref[...], (1, UNIT // LANES))
        yield
        lf = jnp.minimum(g16, 0.0) - jnp.log1p(jnp.exp(-jnp.abs(g16)))
        xs = jnp.concatenate(_split3(lf), axis=0)
        cum = fold(_dot(xs, ucum_ref[...]))
        bch = fold(_dot(xs, ubd_ref[...]))
        tot = fold(_dot(xs, jnp.ones((UNIT, LANES), BF16)))
        c16 = cum + jnp.tile(carry[...], (1, UNIT // LANES))
        carry[...] = carry[...] + tot
        mg_ref[:, rows] = jnp.where(r8 < NH_M, g16[0:8], bch[0:8])
        ys = jnp.concatenate(_split3(-LOG2E * c16), axis=0)
        place = lax.dot_general(ys, e48_ref[...], TN, preferred_element_type=F32)
        yield
        zfq = _dot(u, wnat_ref[:, 1024:1536]) * (DH_F ** -0.5 * LOG2E)
        store_groups(fq_ref, rows, groups(zfq, lambda g: ones_hi if g % 2 else ones_lo))
        yield
        zfk = _dot(u, wnat_ref[:, 1536:2048])
        store_groups(fk_ref, rows, groups(zfk, lambda g: place[:, LANES * g:LANES * (g + 1)]))
        yield

    _skew([unit(i) for i in range(n_units)], lag=N_FF_CHUNKS + 1, early=N_FF_CHUNKS - 2)


def _const_spec(shape):
    nd = len(shape)
    return pl.BlockSpec(shape, lambda *_: (0,) * nd, pipeline_mode=pl.Buffered(1))


def _mix_in(x, n1, wg, wu, wd, nm, wnat, wt, conv, bias16, ucum, ubd, e48):
    B, S, D = x.shape
    n_t = S // TM_IN
    tok = lambda w: pl.BlockSpec((None, TM_IN, w), lambda b, i: (b, i, 0))
    tokt = lambda r: pl.BlockSpec((None, r, TM_IN), lambda b, i: (b, 0, i))
    consts = (n1, wg, wu, wd, nm, wnat, wt, conv, bias16, ucum, ubd, e48)
    out_shape = (
        jax.ShapeDtypeStruct((B, S, D), F32),
        jax.ShapeDtypeStruct((B, S, NH_M * LANES), BF16),
        jax.ShapeDtypeStruct((B, S, NH_M * LANES), BF16),
        jax.ShapeDtypeStruct((B, S, D_MV), BF16),
        jax.ShapeDtypeStruct((B, D_MV, S), BF16),
        jax.ShapeDtypeStruct((B, 8, S), F32),
        jax.ShapeDtypeStruct((B, S, NH_F * LANES), BF16),
        jax.ShapeDtypeStruct((B, S, NH_F * LANES), BF16),
        jax.ShapeDtypeStruct((B, D_FV, S), BF16),
    )
    out_specs = (tok(D), tok(NH_M * LANES), tok(NH_M * LANES), tok(D_MV), tokt(D_MV), tokt(8),
                 tok(NH_F * LANES), tok(NH_F * LANES), tokt(D_FV))
    return pl.pallas_call(
        _mix_in_kernel,
        grid=(B, n_t),
        in_specs=[tok(D)] + [_const_spec(c.shape) for c in consts],
        out_specs=out_specs,
        out_shape=out_shape,
        scratch_shapes=[pltpu.VMEM((TM_IN + 8, 512), F32), pltpu.VMEM((N_GATES, LANES), F32)],
        compiler_params=pltpu.CompilerParams(
            dimension_semantics=("arbitrary", "arbitrary"), vmem_limit_bytes=VMEM_LIMIT),
        name="mix_in",
    )(x, *consts)


def _mlstm_kernel(q_ref, k_ref, vt_ref, mg_ref, mo_ref, gain_ref, o_ref, ct_ref):
    S = q_ref.shape[0]
    L = L_M
    ct_ref[...] = jnp.zeros(ct_ref.shape, F32)
    causal = (lax.broadcasted_iota(jnp.int32, (L, L), 0) <= lax.broadcasted_iota(jnp.int32, (L, L), 1))
    r16 = lax.broadcasted_iota(jnp.int32, (BF16_ROWS, L), 0)
    ones_rows = jnp.where(r16 < 3, 1.0, 0.0).astype(BF16)
    ones16 = jnp.ones((BF16_ROWS, L), BF16)

    def head_chunk(cs, h, ms):
        hs = slice(LANES * h, LANES * (h + 1))
        m = ms[h]
        logi = mg_ref[h:h + 1, pl.ds(cs, L)]
        b = mg_ref[NH_M + h:NH_M + h + 1, pl.ds(cs, L)]
        a = logi - b
        gtot = b[:, L - 1:L]
        m_new = jnp.maximum(gtot + m, gtot + jnp.max(a, axis=1, keepdims=True))
        ms[h] = m_new
        ahi, amid, alo = (p.astype(F32) for p in _split3(a))
        at = jnp.where(r16 == 0, ahi, jnp.where(r16 == 1, amid, jnp.where(r16 == 2, alo, 0.0)))
        acol = lax.dot_general(at.astype(BF16), ones_rows, TN, preferred_element_type=F32)
        q = q_ref[pl.ds(cs, L), hs]
        k = k_ref[pl.ds(cs, L), hs]
        st = lax.dot_general(k, q, NT, preferred_element_type=F32)
        ct = ct_ref[h]
        inter = lax.dot_general(ct.astype(BF16), q, NT, preferred_element_type=F32)
        yield
        e = jnp.where(causal, acol, -jnp.inf)
        mu = jnp.maximum(m, jnp.max(e, axis=0, keepdims=True))
        pt = (jnp.exp(e - mu) * st).astype(BF16)
        vt_aug = jnp.concatenate([vt_ref[hs, pl.ds(cs, L)], ones16], axis=0)
        w = jnp.exp(a + gtot - m_new)
        vtw = (vt_aug.astype(F32) * w).astype(BF16)
        yield
        nd = jnp.exp(m - mu) * inter + _dot(vt_aug, pt)
        ct_ref[h] = jnp.exp(gtot + m - m_new) * ct + _dot(vtw, k)
        yield
        rden = 1.0 / jnp.maximum(jnp.abs(nd[DV_M:DV_M + 1]), jnp.exp(-(b + mu)))
        ht = nd[0:DV_M] * rden
        yt = (ht * lax.rsqrt(jnp.mean(ht * ht, axis=0, keepdims=True) + EPS)
              * jnp.tile(gain_ref[hs, :], (1, L // LANES)))
        og = jax.nn.sigmoid(mo_ref[pl.ds(cs, L), hs].astype(F32))
        o_ref[pl.ds(cs, L), hs] = (yt.T * og).astype(o_ref.dtype)
        yield

    def trip(c, ms):
        ms = list(ms)
        starts = [pl.multiple_of((c * MLSTM_CHUNKS + j) * L, L) for j in range(MLSTM_CHUNKS)]
        _skew([head_chunk(cs, h, ms) for cs in starts for h in range(NH_M)], lag=1, early=0)
        return tuple(ms)

    lax.fori_loop(0, S // (L * MLSTM_CHUNKS), trip, tuple(jnp.zeros((1, 1), F32) for _ in range(NH_M)))


def _mlstm(mq, mk, mvt, mg, mo, gain):
    B, S, _ = mq.shape
    tok = pl.BlockSpec((None, S, NH_M * LANES), lambda b: (b, 0, 0))
    return pl.pallas_call(
        _mlstm_kernel,
        grid=(B,),
        in_specs=[tok, tok,
                  pl.BlockSpec((None, D_MV, S), lambda b: (b, 0, 0)),
                  pl.BlockSpec((None, 8, S), lambda b: (b, 0, 0)),
                  pl.BlockSpec((None, S, D_MV), lambda b: (b, 0, 0)),
                  _const_spec(gain.shape)],
        out_specs=pl.BlockSpec((None, S, D_MV), lambda b: (b, 0, 0)),
        out_shape=jax.ShapeDtypeStruct((B, S, D_MV), BF16),
        scratch_shapes=[pltpu.VMEM((NH_M, DV_M + BF16_ROWS, LANES), F32)],
        compiler_params=pltpu.CompilerParams(
            dimension_semantics=("arbitrary",), vmem_limit_bytes=VMEM_LIMIT),
        name="mlstm",
    )(mq, mk, mvt, mg, mo, gain)


def _fox_kernel(q_ref, k_ref, vt_ref, gain_ref, o_ref, st_a, st_b, mb_a, mb_b, m_ref, acc_ref, bias_ref):
    S = q_ref.shape[0]
    T = T_F
    n_q = S // T
    heads = [slice(LANES * j, LANES * (j + 1)) for j in range(2)]
    bufs = ((st_a, mb_a), (st_b, mb_b))
    n_strips = T // MXU_COLS
    parts = [(j, hc) for j in range(2) for hc in range(n_strips)]
    ones16 = jnp.ones((BF16_ROWS, T), BF16)
    for hc in range(n_strips):
        visible = (lax.broadcasted_iota(jnp.int32, (T, MXU_COLS), 0)
                   <= lax.broadcasted_iota(jnp.int32, (T, MXU_COLS), 1) + hc * MXU_COLS)
        bias_ref[hc] = jnp.where(visible, 0.0, NEG)

    def n_keys(diag, hc):
        return (hc + 1) * MXU_COLS if diag else T

    def logits(c, buf, diag, part):
        qi, kj = c
        st_buf, mb_buf = buf
        j, hc = part
        nk = n_keys(diag, hc)
        qs = pl.multiple_of(qi * T + hc * MXU_COLS, MXU_COLS)
        ks = pl.multiple_of(kj * T, T)
        st = lax.dot_general(k_ref[pl.ds(ks, nk), heads[j]], q_ref[pl.ds(qs, MXU_COLS), heads[j]], NT,
                             preferred_element_type=F32)
        if diag:
            st = st + bias_ref[hc, 0:nk, :]
        st_buf[j, hc, 0:nk, :] = st
        mb_buf[j, hc] = jnp.max(st, axis=0, keepdims=True)

    def update(c, buf, first, part):
        qi, kj = c
        st_buf, mb_buf = buf
        j, hc = part
        nk = n_keys(first, hc)
        ks = pl.multiple_of(kj * T, T)
        m_old = m_ref[qi, j, hc]
        m_new = mb_buf[j, hc] if first else jnp.maximum(m_old, mb_buf[j, hc])
        pt = jnp.exp2(st_buf[j, hc, 0:nk, :] - m_new).astype(BF16)
        vt_aug = jnp.concatenate([vt_ref[DH_F * j:DH_F * (j + 1), pl.ds(ks, nk)], ones16[:, 0:nk]], axis=0)
        pv = _dot(vt_aug, pt)
        acc_ref[qi, j, hc] = pv if first else jnp.exp2(m_old - m_new) * acc_ref[qi, j, hc] + pv
        m_ref[qi, j, hc] = m_new

    def step(cur, nxt, b, diag):
        for part in parts:
            if nxt is not None:
                logits(nxt, bufs[1 - b], diag, part)
            update(cur, bufs[b], diag, part)

    def sweep(n, start, succ, diag):
        if n == 0:
            return
        for part in parts:
            logits(start, bufs[0], diag, part)

        def trip(_, c):
            for i in range(FOX_UNROLL):
                cn = succ(c)
                step(c, cn, i % 2, diag)
                c = cn
            return c

        trips = (n - 1) // FOX_UNROLL
        c = lax.fori_loop(0, trips, trip, start)
        rest = n - trips * FOX_UNROLL
        for i in range(rest):
            cn = succ(c) if i < rest - 1 else None
            step(c, cn, i % 2, diag)
            c = cn

    zero = jnp.int32(0)
    sweep(n_q, (zero, zero), lambda c: (c[0] + 1, c[1] + 1), True)

    def below(c):
        wrap = c[1] + 1 >= c[0]
        return jnp.where(wrap, c[0] + 1, c[0]), jnp.where(wrap, 0, c[1] + 1)

    sweep(n_q * (n_q - 1) // 2, (zero + 1, zero), below, False)

    def finish(qi, _):
        ys = []
        for j in range(2):
            acc = jnp.concatenate([acc_ref[qi, j, hc] for hc in range(n_strips)], axis=1)
            o = acc[0:DH_F] * (1.0 / acc[DH_F:DH_F + 1])
            ys.append(o * lax.rsqrt(jnp.mean(o * o, axis=0, keepdims=True) + EPS)
                      * jnp.tile(gain_ref[DH_F * j:DH_F * (j + 1), :], (1, T // LANES)))
        o_ref[pl.ds(pl.multiple_of(qi * T, T), T), :] = jnp.concatenate(ys, axis=0).T.astype(o_ref.dtype)
        return 0

    lax.fori_loop(0, n_q, finish, 0, unroll=FOX_FINISH_UNROLL)


def _fox(fq, fk, fvt, gain):
    B, S, _ = fq.shape
    n_pair = NH_F // 2
    n_strips = T_F // MXU_COLS
    tok = pl.BlockSpec((None, S, 2 * LANES), lambda b, p: (b, 0, p))
    return pl.pallas_call(
        _fox_kernel,
        grid=(B, n_pair),
        in_specs=[tok, tok,
                  pl.BlockSpec((None, 2 * DH_F, S), lambda b, p: (b, p, 0)),
                  pl.BlockSpec((2 * DH_F, LANES), lambda b, p: (p, 0))],
        out_specs=pl.BlockSpec((None, S, 2 * DH_F), lambda b, p: (b, 0, p)),
        out_shape=jax.ShapeDtypeStruct((B, S, D_FV), BF16),
        scratch_shapes=[pltpu.VMEM((2, n_strips, T_F, MXU_COLS), F32), pltpu.VMEM((2, n_strips, T_F, MXU_COLS), F32),
                        pltpu.VMEM((2, n_strips, 1, MXU_COLS), F32), pltpu.VMEM((2, n_strips, 1, MXU_COLS), F32),
                        pltpu.VMEM((S // T_F, 2, n_strips, 1, MXU_COLS), F32),
                        pltpu.VMEM((S // T_F, 2, n_strips, DH_F + BF16_ROWS, MXU_COLS), F32),
                        pltpu.VMEM((n_strips, T_F, MXU_COLS), F32)],
        compiler_params=pltpu.CompilerParams(
            dimension_semantics=("arbitrary", "arbitrary"), vmem_limit_bytes=VMEM_LIMIT),
        name="fox",
    )(fq, fk, fvt, gain)


def _mix_out_kernel(h_ref, ym_ref, yf_ref, p_ref, wom_ref, wof_ref, n2_ref, wg_ref, wu_ref, wd_ref,
                    ng_ref, wpg_ref, wpp_ref, npp_ref, nf_ref, o_ref):
    def unit(i):
        rows = slice(i * UNIT, (i + 1) * UNIT)
        h = h_ref[rows, :] + _dot(ym_ref[rows, :], wom_ref[...]) + _dot(yf_ref[rows, :], wof_ref[...])
        pe = _rms(_dot(p_ref[rows, :].astype(BF16), wpp_ref[...]), npp_ref[...])
        xn = _rms(h, n2_ref[...]).astype(BF16)
        yield
        acc = None
        for c in range(0, D_FF, FF_CHUNK):
            part = _swiglu_chunk(xn, c, wg_ref, wu_ref, wd_ref)
            acc = part if acc is None else acc + part
            yield
        h = h + 0.5 * acc
        xg = _rms(h, ng_ref[...]).astype(BF16)
        yield
        h = h + jax.nn.sigmoid(_dot(xg, wpg_ref[...])) * pe
        o_ref[rows, :] = _rms(h, nf_ref[...])
        yield

    _skew([unit(i) for i in range(h_ref.shape[0] // UNIT)], lag=N_FF_CHUNKS + 1, early=N_FF_CHUNKS - 2)


def _mix_out(h1, ym, yf, p, *consts):
    B, S, D = h1.shape
    tok = lambda w: pl.BlockSpec((None, TM_OUT, w), lambda b, i: (b, i, 0))
    return pl.pallas_call(
        _mix_out_kernel,
        grid=(B, S // TM_OUT),
        in_specs=[tok(D), tok(D_MV), tok(D_FV), tok(p.shape[-1])] + [_const_spec(c.shape) for c in consts],
        out_specs=tok(D),
        out_shape=jax.ShapeDtypeStruct((B, S, D), F32),
        compiler_params=pltpu.CompilerParams(
            dimension_semantics=("arbitrary", "arbitrary"), vmem_limit_bytes=VMEM_LIMIT),
        name="mix_out",
    )(h1, ym, yf, p, *consts)


_O_MQK, _O_MV, _O_MO, _O_MIF = 0, 512, 1024, 1536
_O_FQ, _O_FK, _O_FV, _O_FF = 1544, 2056, 2568, 3080
_NAT_COLS = np.concatenate([np.arange(_O_MQK, _O_MQK + 2 * NH_M * DK_M), np.arange(_O_MO, _O_MO + D_MV),
                            np.arange(_O_FQ, _O_FQ + D_FV), np.arange(_O_FK, _O_FK + D_FV)])
_T_COLS = np.concatenate([np.arange(_O_MV, _O_MV + D_MV), np.arange(_O_FV, _O_FV + D_FV),
                          np.arange(_O_MIF, _O_MIF + 2 * NH_M), np.arange(_O_FF, _O_FF + NH_F)])


def _placement():
    e = np.zeros((3 * N_GATES, NH_F * LANES), np.float32)
    for j in range(3):
        for h in range(NH_F):
            e[N_GATES * j + 2 * NH_M + h, LANES * h + (j if h % 2 else DH_F + j)] = 1.0
    return e


def _tri(n, block):
    s = np.arange(n)[:, None]
    t = np.arange(n)[None, :]
    return ((s <= t) & (s // block == t // block)).astype(np.float32)


def _take_cols(w, cols):
    cuts = [0] + [i for i in range(1, len(cols)) if cols[i] != cols[i - 1] + 1] + [len(cols)]
    return jnp.concatenate([w[:, int(cols[a]):int(cols[b - 1]) + 1] for a, b in zip(cuts[:-1], cuts[1:])], axis=1)


def kernel(x, p, ffn1_norm, ffn1_w_gate, ffn1_w_up, ffn1_w_down, mix_norm, w_in, conv_qk, b_mlstm_gates, b_fox_f, mlstm_out_norm, fox_out_norm, w_out, ffn2_norm, ffn2_w_gate, ffn2_w_up, ffn2_w_down, ple_gate_norm, w_ple_gate, w_ple_proj, ple_proj_norm, final_norm):
    depth = w_in.shape[0]
    assert depth == 1, "the final RMSNorm is fused into the (single) layer's last kernel"
    assert x.shape[1] % TM_OUT == 0 and x.shape[1] % (L_M * MLSTM_CHUNKS) == 0 and x.shape[2] == D_MODEL
    assert TM_IN % UNIT == 0 and TM_OUT % UNIT == 0 and UNIT % L_M == 0 and FOX_UNROLL % 2 == 0
    row = lambda v: v.reshape(1, -1).astype(F32)
    col = lambda v: jnp.broadcast_to(v.astype(F32)[:, None], (v.shape[0], LANES))
    ucum = jnp.asarray(_tri(UNIT, UNIT), BF16)
    ubd = jnp.asarray(_tri(UNIT, L_M), BF16)
    e48 = jnp.asarray(_placement(), BF16)
    h = x
    for i in range(depth):
        wnat = _take_cols(w_in[i], _NAT_COLS).astype(BF16)
        wt = _take_cols(w_in[i], _T_COLS).T.astype(BF16)
        conv = conv_qk[i].astype(F32)
        bias16 = col(jnp.concatenate([b_mlstm_gates[i], b_fox_f[i]]))
        h1, mq, mk, mo, mvt, mg, fq, fk, fvt = _mix_in(
            h, row(ffn1_norm[i]), ffn1_w_gate[i].astype(BF16), ffn1_w_up[i].astype(BF16),
            ffn1_w_down[i].astype(BF16), row(mix_norm[i]), wnat, wt, conv, bias16, ucum, ubd, e48)
        ym = _mlstm(mq, mk, mvt, mg, mo, col(mlstm_out_norm[i]))
        yf = _fox(fq, fk, fvt, col(fox_out_norm[i]))
        wo = w_out[i].astype(BF16)
        h = _mix_out(
            h1, ym, yf, p[i], wo[:D_MV], wo[D_MV:], row(ffn2_norm[i]), ffn2_w_gate[i].astype(BF16),
            ffn2_w_up[i].astype(BF16), ffn2_w_down[i].astype(BF16), row(ple_gate_norm[i]),
            w_ple_gate[i].astype(BF16), w_ple_proj[i].astype(BF16), row(ple_proj_norm[i]),
            row(final_norm))
    return h
```

```python
import numpy as np
import jax
import jax.numpy as jnp
from jax import lax
from jax.experimental import pallas as pl
from jax.experimental.pallas import tpu as pltpu

F32 = jnp.float32
BF16 = jnp.bfloat16

EPS = 1e-6
D_MODEL = 1024
D_FF = 2816
NH_M = 4
DV_M = 128
DK_M = 64
CONV_W = 4
NH_F = 8
DH_F = 64
D_MV = NH_M * DV_M
D_FV = NH_F * DH_F
D_MQK = 2 * NH_M * DK_M
N_GATES = 2 * NH_M + NH_F

LANES = 128
SUBLANES = 8
BF16_ROWS = 16
MXU_COLS = 256
TM_IN = 512
TM_OUT = 1024
UNIT = 512
FF_CHUNK = 256
N_FF_CHUNKS = D_FF // FF_CHUNK
L_M = 256
MLSTM_CHUNKS = 4
T_F = 512
LOG2E = 1.4426950408889634
FOX_UNROLL = 8
FOX_FINISH_UNROLL = 4
NEG = -1e30
VMEM_LIMIT = 60 * 1024 * 1024

NT = (((1,), (1,)), ((), ()))
TN = (((0,), (0,)), ((), ()))
TT = (((0,), (1,)), ((), ()))


def _rms(x, g):
    return x * lax.rsqrt(jnp.mean(x * x, axis=-1, keepdims=True) + EPS) * g


def _dot(a, b):
    return jnp.dot(a, b, preferred_element_type=F32)


def _split3(x):
    hi = x.astype(BF16)
    r = x - hi.astype(F32)
    mid = r.astype(BF16)
    lo = (r - mid.astype(F32)).astype(BF16)
    return hi, mid, lo


def _swiglu_chunk(xn, c, wg_ref, wu_ref, wd_ref):
    a = _dot(xn, wg_ref[:, c:c + FF_CHUNK])
    b = _dot(xn, wu_ref[:, c:c + FF_CHUNK])
    return _dot((a * jax.nn.sigmoid(a) * b).astype(BF16), wd_ref[c:c + FF_CHUNK, :])


def _skew(units, lag, early):
    done = [0] * len(units)
    tick = 0
    while any(d >= 0 for d in done):
        for i, u in enumerate(units):
            due = max(0, i * lag - early) if done[i] == 0 else i * lag + done[i]
            if done[i] >= 0 and tick >= due:
                try:
                    next(u)
                    done[i] += 1
                except StopIteration:
                    done[i] = -1
        tick += 1


def _mix_in_kernel(x_ref, n1_ref, wg_ref, wu_ref, wd_ref, nm_ref, wnat_ref, wt_ref, conv_ref,
                   bias_ref, ucum_ref, ubd_ref, e48_ref,
                   h1_ref, mq_ref, mk_ref, mo_ref, mvt_ref, mg_ref, fq_ref, fk_ref, fvt_ref,
                   cbuf, carry):
    tm = x_ref.shape[0]
    n_units = tm // UNIT

    @pl.when(pl.program_id(1) == 0)
    def _():
        cbuf[0:SUBLANES, :] = jnp.zeros((SUBLANES, cbuf.shape[1]), F32)
        carry[...] = jnp.zeros(carry.shape, F32)

    lane = lax.broadcasted_iota(jnp.int32, (UNIT, LANES), 1)
    low = lane < DK_M
    ones_lo = jnp.where((lane >= DH_F) & (lane < DH_F + 3), 1.0, 0.0)
    ones_hi = jnp.where(lane < 3, 1.0, 0.0)
    r8 = lax.broadcasted_iota(jnp.int32, (2 * NH_M, UNIT), 0)

    def groups(z, fill):
        out = []
        for j in range(z.shape[1] // LANES):
            c = z[:, LANES * j:LANES * (j + 1)]
            out += [jnp.where(low, c, fill(2 * j)), jnp.where(low, fill(2 * j + 1), c)]
        return out

    def store_groups(ref, rows, gs):
        for g, v in enumerate(gs):
            ref[rows, LANES * g:LANES * (g + 1)] = v.astype(ref.dtype)

    def fold(r):
        return r[0:N_GATES] + r[N_GATES:2 * N_GATES] + r[2 * N_GATES:3 * N_GATES]

    zero = lambda j: 0.0

    def unit(i):
        r0 = i * UNIT
        rows = slice(r0, r0 + UNIT)
        x = x_ref[rows, :]
        xn = _rms(x, n1_ref[...]).astype(BF16)
        yield
        acc = None
        for c in range(0, D_FF, FF_CHUNK):
            part = _swiglu_chunk(xn, c, wg_ref, wu_ref, wd_ref)
            acc = part if acc is None else acc + part
            yield
        h1 = x + 0.5 * acc
        h1_ref[rows, :] = h1
        u = _rms(h1, nm_ref[...]).astype(BF16)
        yield

        zqk = _dot(u, wnat_ref[:, 0:D_MQK])
        cbuf[SUBLANES + r0:SUBLANES + r0 + UNIT, :] = zqk
        yield
        cv = conv_ref[CONV_W - 1:CONV_W, :] * zqk
        for j in range(CONV_W - 1):
            top = SUBLANES + r0 - (CONV_W - 1) + j
            cv = cv + conv_ref[j:j + 1, :] * cbuf[top:top + UNIT, :]
        if i == n_units - 1:
            cbuf[0:SUBLANES, :] = cbuf[tm:tm + SUBLANES, :]
        s = cv * jax.nn.sigmoid(cv)
        store_groups(mq_ref, rows, groups(s[:, 0:D_MQK // 2] * (DK_M ** -0.5), zero))
        store_groups(mk_ref, rows, groups(s[:, D_MQK // 2:D_MQK], zero))
        yield

        mo_ref[rows, :] = _dot(u, wnat_ref[:, D_MQK:D_MQK + D_MV]).astype(BF16)
        yield

        zt = lax.dot_general(wt_ref[...], u, TT, preferred_element_type=F32)
        mvt_ref[:, rows] = zt[0:D_MV].astype(BF16)
        fvt_ref[:, rows] = zt[D_MV:D_MV + D_FV].astype(BF16)
        g16 = zt[D_MV + D_FV:D_MV + D_FV + N_GATES] + jnp.tile(bias_ref[...], (1, UNIT // LANES))
        yield
        lf = jnp.minimum(g16, 0.0) - jnp.log1p(jnp.exp(-jnp.abs(g16)))
        xs = jnp.concatenate(_split3(lf), axis=0)
        cum = fold(_dot(xs, ucum_ref[...]))
        bch = fold(_dot(xs, ubd_ref[...]))
        tot = fold(_dot(xs, jnp.ones((UNIT, LANES), BF16)))
        c16 = cum + jnp.tile(carry[...], (1, UNIT // LANES))
        carry[...] = carry[...] + tot
        mg_ref[:, rows] = jnp.where(r8 < NH_M, g16[0:2 * NH_M], bch[0:2 * NH_M])
        ys = jnp.concatenate(_split3(-LOG2E * c16), axis=0)
        place = lax.dot_general(ys, e48_ref[...], TN, preferred_element_type=F32)
        yield
        zfq = _dot(u, wnat_ref[:, D_MQK + D_MV:D_MQK + D_MV + D_FV]) * (DH_F ** -0.5 * LOG2E)
        store_groups(fq_ref, rows, groups(zfq, lambda g: ones_hi if g % 2 else ones_lo))
        yield
        zfk = _dot(u, wnat_ref[:, D_MQK + D_MV + D_FV:D_MQK + D_MV + 2 * D_FV])
        store_groups(fk_ref, rows, groups(zfk, lambda g: place[:, LANES * g:LANES * (g + 1)]))
        yield

    _skew([unit(i) for i in range(n_units)], lag=N_FF_CHUNKS + 1, early=N_FF_CHUNKS - 2)


def _const_spec(shape):
    nd = len(shape)
    return pl.BlockSpec(shape, lambda *_: (0,) * nd, pipeline_mode=pl.Buffered(1))


def _mix_in(x, n1, wg, wu, wd, nm, wnat, wt, conv, bias16, ucum, ubd, e48):
    B, S, D = x.shape
    n_t = S // TM_IN
    tok = lambda w: pl.BlockSpec((None, TM_IN, w), lambda b, i: (b, i, 0))
    tokt = lambda r: pl.BlockSpec((None, r, TM_IN), lambda b, i: (b, 0, i))
    consts = (n1, wg, wu, wd, nm, wnat, wt, conv, bias16, ucum, ubd, e48)
    out_shape = (
        jax.ShapeDtypeStruct((B, S, D), F32),
        jax.ShapeDtypeStruct((B, S, NH_M * LANES), BF16),
        jax.ShapeDtypeStruct((B, S, NH_M * LANES), BF16),
        jax.ShapeDtypeStruct((B, S, D_MV), BF16),
        jax.ShapeDtypeStruct((B, D_MV, S), BF16),
        jax.ShapeDtypeStruct((B, 2 * NH_M, S), F32),
        jax.ShapeDtypeStruct((B, S, NH_F * LANES), BF16),
        jax.ShapeDtypeStruct((B, S, NH_F * LANES), BF16),
        jax.ShapeDtypeStruct((B, D_FV, S), BF16),
    )
    out_specs = (tok(D), tok(NH_M * LANES), tok(NH_M * LANES), tok(D_MV), tokt(D_MV), tokt(2 * NH_M),
                 tok(NH_F * LANES), tok(NH_F * LANES), tokt(D_FV))
    return pl.pallas_call(
        _mix_in_kernel,
        grid=(B, n_t),
        in_specs=[tok(D)] + [_const_spec(c.shape) for c in consts],
        out_specs=out_specs,
        out_shape=out_shape,
        scratch_shapes=[pltpu.VMEM((TM_IN + SUBLANES, D_MQK), F32), pltpu.VMEM((N_GATES, LANES), F32)],
        compiler_params=pltpu.CompilerParams(
            dimension_semantics=("arbitrary", "arbitrary"), vmem_limit_bytes=VMEM_LIMIT),
        name="mix_in",
    )(x, *consts)


def _mlstm_kernel(q_ref, k_ref, vt_ref, mg_ref, mo_ref, gain_ref, o_ref, ct_ref):
    S = q_ref.shape[0]
    L = L_M
    ct_ref[...] = jnp.zeros(ct_ref.shape, F32)
    causal = (lax.broadcasted_iota(jnp.int32, (L, L), 0) <= lax.broadcasted_iota(jnp.int32, (L, L), 1))
    r16 = lax.broadcasted_iota(jnp.int32, (BF16_ROWS, L), 0)
    ones_rows = jnp.where(r16 < 3, 1.0, 0.0).astype(BF16)
    ones16 = jnp.ones((BF16_ROWS, L), BF16)

    def head_chunk(cs, h, ms):
        hs = slice(LANES * h, LANES * (h + 1))
        m = ms[h]
        logi = mg_ref[h:h + 1, pl.ds(cs, L)]
        b = mg_ref[NH_M + h:NH_M + h + 1, pl.ds(cs, L)]
        a = logi - b
        gtot = b[:, L - 1:L]
        m_new = jnp.maximum(gtot + m, gtot + jnp.max(a, axis=1, keepdims=True))
        ms[h] = m_new
        ahi, amid, alo = (p.astype(F32) for p in _split3(a))
        at = jnp.where(r16 == 0, ahi, jnp.where(r16 == 1, amid, jnp.where(r16 == 2, alo, 0.0)))
        acol = lax.dot_general(at.astype(BF16), ones_rows, TN, preferred_element_type=F32)
        q = q_ref[pl.ds(cs, L), hs]
        k = k_ref[pl.ds(cs, L), hs]
        st = lax.dot_general(k, q, NT, preferred_element_type=F32)
        ct = ct_ref[h]
        inter = lax.dot_general(ct.astype(BF16), q, NT, preferred_element_type=F32)
        yield
        e = jnp.where(causal, acol, -jnp.inf)
        mu = jnp.maximum(m, jnp.max(e, axis=0, keepdims=True))
        pt = (jnp.exp(e - mu) * st).astype(BF16)
        vt_aug = jnp.concatenate([vt_ref[hs, pl.ds(cs, L)], ones16], axis=0)
        w = jnp.exp(a + gtot - m_new)
        vtw = (vt_aug.astype(F32) * w).astype(BF16)
        yield
        nd = jnp.exp(m - mu) * inter + _dot(vt_aug, pt)
        ct_ref[h] = jnp.exp(gtot + m - m_new) * ct + _dot(vtw, k)
        yield
        rden = 1.0 / jnp.maximum(jnp.abs(nd[DV_M:DV_M + 1]), jnp.exp(-(b + mu)))
        ht = nd[0:DV_M] * rden
        yt = (ht * lax.rsqrt(jnp.mean(ht * ht, axis=0, keepdims=True) + EPS)
              * jnp.tile(gain_ref[hs, :], (1, L // LANES)))
        og = jax.nn.sigmoid(mo_ref[pl.ds(cs, L), hs].astype(F32))
        o_ref[pl.ds(cs, L), hs] = (yt.T * og).astype(o_ref.dtype)
        yield

    def trip(c, ms):
        ms = list(ms)
        starts = [pl.multiple_of((c * MLSTM_CHUNKS + j) * L, L) for j in range(MLSTM_CHUNKS)]
        _skew([head_chunk(cs, h, ms) for cs in starts for h in range(NH_M)], lag=1, early=0)
        return tuple(ms)

    lax.fori_loop(0, S // (L * MLSTM_CHUNKS), trip, tuple(jnp.zeros((1, 1), F32) for _ in range(NH_M)))


def _mlstm(mq, mk, mvt, mg, mo, gain):
    B, S, _ = mq.shape
    tok = pl.BlockSpec((None, S, NH_M * LANES), lambda b: (b, 0, 0))
    return pl.pallas_call(
        _mlstm_kernel,
        grid=(B,),
        in_specs=[tok, tok,
                  pl.BlockSpec((None, D_MV, S), lambda b: (b, 0, 0)),
                  pl.BlockSpec((None, 2 * NH_M, S), lambda b: (b, 0, 0)),
                  pl.BlockSpec((None, S, D_MV), lambda b: (b, 0, 0)),
                  _const_spec(gain.shape)],
        out_specs=pl.BlockSpec((None, S, D_MV), lambda b: (b, 0, 0)),
        out_shape=jax.ShapeDtypeStruct((B, S, D_MV), BF16),
        scratch_shapes=[pltpu.VMEM((NH_M, DV_M + BF16_ROWS, LANES), F32)],
        compiler_params=pltpu.CompilerParams(
            dimension_semantics=("arbitrary",), vmem_limit_bytes=VMEM_LIMIT),
        name="mlstm",
    )(mq, mk, mvt, mg, mo, gain)


def _fox_kernel(q_ref, k_ref, vt_ref, gain_ref, o_ref, st_a, st_b, mb_a, mb_b, m_ref, acc_ref, bias_ref):
    S = q_ref.shape[0]
    T = T_F
    n_q = S // T
    heads = [slice(LANES * j, LANES * (j + 1)) for j in range(2)]
    bufs = ((st_a, mb_a), (st_b, mb_b))
    n_strips = T // MXU_COLS
    parts = [(j, hc) for j in range(2) for hc in range(n_strips)]
    ones16 = jnp.ones((BF16_ROWS, T), BF16)
    for hc in range(n_strips):
        visible = (lax.broadcasted_iota(jnp.int32, (T, MXU_COLS), 0)
                   <= lax.broadcasted_iota(jnp.int32, (T, MXU_COLS), 1) + hc * MXU_COLS)
        bias_ref[hc] = jnp.where(visible, 0.0, NEG)

    def n_keys(diag, hc):
        return (hc + 1) * MXU_COLS if diag else T

    def logits(c, buf, diag, part):
        qi, kj = c
        st_buf, mb_buf = buf
        j, hc = part
        nk = n_keys(diag, hc)
        qs = pl.multiple_of(qi * T + hc * MXU_COLS, MXU_COLS)
        ks = pl.multiple_of(kj * T, T)
        st = lax.dot_general(k_ref[pl.ds(ks, nk), heads[j]], q_ref[pl.ds(qs, MXU_COLS), heads[j]], NT,
                             preferred_element_type=F32)
        if diag:
            st = st + bias_ref[hc, 0:nk, :]
        st_buf[j, hc, 0:nk, :] = st
        mb_buf[j, hc] = jnp.max(st, axis=0, keepdims=True)

    def update(c, buf, first, part):
        qi, kj = c
        st_buf, mb_buf = buf
        j, hc = part
        nk = n_keys(first, hc)
        ks = pl.multiple_of(kj * T, T)
        m_old = m_ref[qi, j, hc]
        m_new = mb_buf[j, hc] if first else jnp.maximum(m_old, mb_buf[j, hc])
        pt = jnp.exp2(st_buf[j, hc, 0:nk, :] - m_new).astype(BF16)
        vt_aug = jnp.concatenate([vt_ref[DH_F * j:DH_F * (j + 1), pl.ds(ks, nk)], ones16[:, 0:nk]], axis=0)
        pv = _dot(vt_aug, pt)
        acc_ref[qi, j, hc] = pv if first else jnp.exp2(m_old - m_new) * acc_ref[qi, j, hc] + pv
        m_ref[qi, j, hc] = m_new

    def step(cur, nxt, b, diag):
        for part in parts:
            if nxt is not None:
                logits(nxt, bufs[1 - b], diag, part)
            update(cur, bufs[b], diag, part)

    def sweep(n, start, succ, diag):
        if n == 0:
            return
        for part in parts:
            logits(start, bufs[0], diag, part)

        def trip(_, c):
            for i in range(FOX_UNROLL):
                cn = succ(c)
                step(c, cn, i % 2, diag)
                c = cn
            return c

        trips = (n - 1) // FOX_UNROLL
        c = lax.fori_loop(0, trips, trip, start)
        rest = n - trips * FOX_UNROLL
        for i in range(rest):
            cn = succ(c) if i < rest - 1 else None
            step(c, cn, i % 2, diag)
            c = cn

    zero = jnp.int32(0)
    sweep(n_q, (zero, zero), lambda c: (c[0] + 1, c[1] + 1), True)

    def below(c):
        wrap = c[1] + 1 >= c[0]
        return jnp.where(wrap, c[0] + 1, c[0]), jnp.where(wrap, 0, c[1] + 1)

    sweep(n_q * (n_q - 1) // 2, (zero + 1, zero), below, False)

    def finish(qi, _):
        ys = []
        for j in range(2):
            acc = jnp.concatenate([acc_ref[qi, j, hc] for hc in range(n_strips)], axis=1)
            o = acc[0:DH_F] * (1.0 / acc[DH_F:DH_F + 1])
            ys.append(o * lax.rsqrt(jnp.mean(o * o, axis=0, keepdims=True) + EPS)
                      * jnp.tile(gain_ref[DH_F * j:DH_F * (j + 1), :], (1, T // LANES)))
        o_ref[pl.ds(pl.multiple_of(qi * T, T), T), :] = jnp.concatenate(ys, axis=0).T.astype(o_ref.dtype)
        return 0

    lax.fori_loop(0, n_q, finish, 0, unroll=FOX_FINISH_UNROLL)


def _fox(fq, fk, fvt, gain):
    B, S, _ = fq.shape
    n_pair = NH_F // 2
    n_strips = T_F // MXU_COLS
    tok = pl.BlockSpec((None, S, 2 * LANES), lambda b, p: (b, 0, p))
    return pl.pallas_call(
        _fox_kernel,
        grid=(B, n_pair),
        in_specs=[tok, tok,
                  pl.BlockSpec((None, 2 * DH_F, S), lambda b, p: (b, p, 0)),
                  pl.BlockSpec((2 * DH_F, LANES), lambda b, p: (p, 0))],
        out_specs=pl.BlockSpec((None, S, 2 * DH_F), lambda b, p: (b, 0, p)),
        out_shape=jax.ShapeDtypeStruct((B, S, D_FV), BF16),
        scratch_shapes=[pltpu.VMEM((2, n_strips, T_F, MXU_COLS), F32), pltpu.VMEM((2, n_strips, T_F, MXU_COLS), F32),
                        pltpu.VMEM((2, n_strips, 1, MXU_COLS), F32), pltpu.VMEM((2, n_strips, 1, MXU_COLS), F32),
                        pltpu.VMEM((S // T_F, 2, n_strips, 1, MXU_COLS), F32),
                        pltpu.VMEM((S // T_F, 2, n_strips, DH_F + BF16_ROWS, MXU_COLS), F32),
                        pltpu.VMEM((n_strips, T_F, MXU_COLS), F32)],
        compiler_params=pltpu.CompilerParams(
            dimension_semantics=("arbitrary", "arbitrary"), vmem_limit_bytes=VMEM_LIMIT),
        name="fox",
    )(fq, fk, fvt, gain)


def _mix_out_kernel(h_ref, ym_ref, yf_ref, p_ref, wom_ref, wof_ref, n2_ref, wg_ref, wu_ref, wd_ref,
                    ng_ref, wpg_ref, wpp_ref, npp_ref, nf_ref, o_ref):
    def unit(i):
        rows = slice(i * UNIT, (i + 1) * UNIT)
        h = h_ref[rows, :] + _dot(ym_ref[rows, :], wom_ref[...]) + _dot(yf_ref[rows, :], wof_ref[...])
        pe = _rms(_dot(p_ref[rows, :].astype(BF16), wpp_ref[...]), npp_ref[...])
        xn = _rms(h, n2_ref[...]).astype(BF16)
        yield
        acc = None
        for c in range(0, D_FF, FF_CHUNK):
            part = _swiglu_chunk(xn, c, wg_ref, wu_ref, wd_ref)
            acc = part if acc is None else acc + part
            yield
        h = h + 0.5 * acc
        xg = _rms(h, ng_ref[...]).astype(BF16)
        yield
        h = h + jax.nn.sigmoid(_dot(xg, wpg_ref[...])) * pe
        o_ref[rows, :] = _rms(h, nf_ref[...])
        yield

    _skew([unit(i) for i in range(h_ref.shape[0] // UNIT)], lag=N_FF_CHUNKS + 1, early=N_FF_CHUNKS - 2)


def _mix_out(h1, ym, yf, p, *consts):
    B, S, D = h1.shape
    tok = lambda w: pl.BlockSpec((None, TM_OUT, w), lambda b, i: (b, i, 0))
    return pl.pallas_call(
        _mix_out_kernel,
        grid=(B, S // TM_OUT),
        in_specs=[tok(D), tok(D_MV), tok(D_FV), tok(p.shape[-1])] + [_const_spec(c.shape) for c in consts],
        out_specs=tok(D),
        out_shape=jax.ShapeDtypeStruct((B, S, D), F32),
        compiler_params=pltpu.CompilerParams(
            dimension_semantics=("arbitrary", "arbitrary"), vmem_limit_bytes=VMEM_LIMIT),
        name="mix_out",
    )(h1, ym, yf, p, *consts)


_O_MQK = 0
_O_MV = _O_MQK + D_MQK
_O_MO = _O_MV + D_MV
_O_MIF = _O_MO + D_MV
_O_FQ = _O_MIF + 2 * NH_M
_O_FK = _O_FQ + D_FV
_O_FV = _O_FK + D_FV
_O_FF = _O_FV + D_FV
_NAT_COLS = np.concatenate([np.arange(_O_MQK, _O_MQK + D_MQK), np.arange(_O_MO, _O_MO + D_MV),
                            np.arange(_O_FQ, _O_FQ + D_FV), np.arange(_O_FK, _O_FK + D_FV)])
_T_COLS = np.concatenate([np.arange(_O_MV, _O_MV + D_MV), np.arange(_O_FV, _O_FV + D_FV),
                          np.arange(_O_MIF, _O_MIF + 2 * NH_M), np.arange(_O_FF, _O_FF + NH_F)])


def _placement():
    e = np.zeros((3 * N_GATES, NH_F * LANES), np.float32)
    for j in range(3):
        for h in range(NH_F):
            e[N_GATES * j + 2 * NH_M + h, LANES * h + (j if h % 2 else DH_F + j)] = 1.0
    return e


def _tri(n, block):
    s = np.arange(n)[:, None]
    t = np.arange(n)[None, :]
    return ((s <= t) & (s // block == t // block)).astype(np.float32)


def _take_cols(w, cols):
    cuts = [0] + [i for i in range(1, len(cols)) if cols[i] != cols[i - 1] + 1] + [len(cols)]
    return jnp.concatenate([w[:, int(cols[a]):int(cols[b - 1]) + 1] for a, b in zip(cuts[:-1], cuts[1:])], axis=1)


def kernel(x, p, ffn1_norm, ffn1_w_gate, ffn1_w_up, ffn1_w_down, mix_norm, w_in, conv_qk, b_mlstm_gates, b_fox_f, mlstm_out_norm, fox_out_norm, w_out, ffn2_norm, ffn2_w_gate, ffn2_w_up, ffn2_w_down, ple_gate_norm, w_ple_gate, w_ple_proj, ple_proj_norm, final_norm):
    depth = w_in.shape[0]
    assert depth == 1, "the final RMSNorm is fused into the (single) layer's last kernel"
    assert x.shape[1] % TM_OUT == 0 and x.shape[1] % (L_M * MLSTM_CHUNKS) == 0 and x.shape[2] == D_MODEL
    assert TM_IN % UNIT == 0 and TM_OUT % UNIT == 0 and UNIT % L_M == 0 and FOX_UNROLL % 2 == 0
    row = lambda v: v.reshape(1, -1).astype(F32)
    col = lambda v: jnp.broadcast_to(v.astype(F32)[:, None], (v.shape[0], LANES))
    ucum = jnp.asarray(_tri(UNIT, UNIT), BF16)
    ubd = jnp.asarray(_tri(UNIT, L_M), BF16)
    e48 = jnp.asarray(_placement(), BF16)
    h = x
    for i in range(depth):
        wnat = _take_cols(w_in[i], _NAT_COLS).astype(BF16)
        wt = _take_cols(w_in[i], _T_COLS).astype(BF16)
        conv = conv_qk[i].astype(F32)
        bias16 = col(jnp.concatenate([b_mlstm_gates[i], b_fox_f[i]]))
        h1, mq, mk, mo, mvt, mg, fq, fk, fvt = _mix_in(
            h, row(ffn1_norm[i]), ffn1_w_gate[i].astype(BF16), ffn1_w_up[i].astype(BF16),
            ffn1_w_down[i].astype(BF16), row(mix_norm[i]), wnat, wt, conv, bias16, ucum, ubd, e48)
        ym = _mlstm(mq, mk, mvt, mg, mo, col(mlstm_out_norm[i]))
        yf = _fox(fq, fk, fvt, col(fox_out_norm[i]))
        wo = w_out[i].astype(BF16)
        h = _mix_out(
            h1, ym, yf, p[i], wo[:D_MV], wo[D_MV:], row(ffn2_norm[i]), ffn2_w_gate[i].astype(BF16),
            ffn2_w_up[i].astype(BF16), ffn2_w_down[i].astype(BF16), row(ple_gate_norm[i]),
            w_ple_gate[i].astype(BF16), w_ple_proj[i].astype(BF16), row(ple_proj_norm[i]),
            row(final_norm))
    return h
```

```python
import numpy as np
import jax
import jax.numpy as jnp
from jax import lax
from jax.experimental import pallas as pl
from jax.experimental.pallas import tpu as pltpu

F32 = jnp.float32
BF16 = jnp.bfloat16

EPS = 1e-6
D_MODEL = 1024
D_FF = 2816
NH_M = 4
DV_M = 128
DK_M = 64
CONV_W = 4
NH_F = 8
DH_F = 64
D_MV = NH_M * DV_M
D_FV = NH_F * DH_F
D_MQK = 2 * NH_M * DK_M
N_GATES = 2 * NH_M + NH_F

LANES = 128
SUBLANES = 8
BF16_ROWS = 16
MXU_COLS = 256
TM_IN = 512
TM_OUT = 1024
UNIT = 512
FF_CHUNK = 256
N_FF_CHUNKS = D_FF // FF_CHUNK
L_M = 256
MLSTM_CHUNKS = 4
T_F = 512
LOG2E = 1.4426950408889634
FOX_UNROLL = 8
FOX_FINISH_UNROLL = 4
NEG = -1e30
VMEM_LIMIT = 60 * 1024 * 1024

NT = (((1,), (1,)), ((), ()))
TN = (((0,), (0,)), ((), ()))


def _rms(x, g):
    return x * lax.rsqrt(jnp.mean(x * x, axis=-1, keepdims=True) + EPS) * g


def _dot(a, b):
    return jnp.dot(a, b, preferred_element_type=F32)


def _split3(x):
    hi = x.astype(BF16)
    r = x - hi.astype(F32)
    mid = r.astype(BF16)
    lo = (r - mid.astype(F32)).astype(BF16)
    return hi, mid, lo


def _swiglu_chunk(xn, c, wg_ref, wu_ref, wd_ref):
    a = _dot(xn, wg_ref[:, c:c + FF_CHUNK])
    b = _dot(xn, wu_ref[:, c:c + FF_CHUNK])
    return _dot((a * jax.nn.sigmoid(a) * b).astype(BF16), wd_ref[c:c + FF_CHUNK, :])


def _skew(units, lag, early):
    done = [0] * len(units)
    tick = 0
    while any(d >= 0 for d in done):
        for i, u in enumerate(units):
            due = max(0, i * lag - early) if done[i] == 0 else i * lag + done[i]
            if done[i] >= 0 and tick >= due:
                try:
                    next(u)
                    done[i] += 1
                except StopIteration:
                    done[i] = -1
        tick += 1


def _mix_in_kernel(x_ref, n1_ref, wg_ref, wu_ref, wd_ref, nm_ref, wnat_ref, wt_ref, conv_ref,
                   bias_ref, ucum_ref, ubd_ref, e48_ref,
                   h1_ref, mq_ref, mk_ref, mo_ref, mvt_ref, mg_ref, fq_ref, fk_ref, fvt_ref,
                   cbuf, carry):
    tm = x_ref.shape[0]
    n_units = tm // UNIT

    @pl.when(pl.program_id(1) == 0)
    def _():
        cbuf[0:SUBLANES, :] = jnp.zeros((SUBLANES, cbuf.shape[1]), F32)
        carry[...] = jnp.zeros(carry.shape, F32)

    lane = lax.broadcasted_iota(jnp.int32, (UNIT, LANES), 1)
    low = lane < DK_M
    ones_lo = jnp.where((lane >= DH_F) & (lane < DH_F + 3), 1.0, 0.0)
    ones_hi = jnp.where(lane < 3, 1.0, 0.0)
    r8 = lax.broadcasted_iota(jnp.int32, (2 * NH_M, UNIT), 0)

    def groups(z, fill):
        out = []
        for j in range(z.shape[1] // LANES):
            c = z[:, LANES * j:LANES * (j + 1)]
            out += [jnp.where(low, c, fill(2 * j)), jnp.where(low, fill(2 * j + 1), c)]
        return out

    def store_groups(ref, rows, gs):
        for g, v in enumerate(gs):
            ref[rows, LANES * g:LANES * (g + 1)] = v.astype(ref.dtype)

    def fold(r):
        return r[0:N_GATES] + r[N_GATES:2 * N_GATES] + r[2 * N_GATES:3 * N_GATES]

    zero = lambda j: 0.0

    def unit(i):
        r0 = i * UNIT
        rows = slice(r0, r0 + UNIT)
        x = x_ref[rows, :]
        xn = _rms(x, n1_ref[...]).astype(BF16)
        yield
        acc = None
        for c in range(0, D_FF, FF_CHUNK):
            part = _swiglu_chunk(xn, c, wg_ref, wu_ref, wd_ref)
            acc = part if acc is None else acc + part
            yield
        h1 = x + 0.5 * acc
        h1_ref[rows, :] = h1
        u = _rms(h1, nm_ref[...]).astype(BF16)
        yield

        zqk = _dot(u, wnat_ref[:, 0:D_MQK])
        cbuf[SUBLANES + r0:SUBLANES + r0 + UNIT, :] = zqk
        yield
        cv = conv_ref[CONV_W - 1:CONV_W, :] * zqk
        for j in range(CONV_W - 1):
            top = SUBLANES + r0 - (CONV_W - 1) + j
            cv = cv + conv_ref[j:j + 1, :] * cbuf[top:top + UNIT, :]
        if i == n_units - 1:
            cbuf[0:SUBLANES, :] = cbuf[tm:tm + SUBLANES, :]
        s = cv * jax.nn.sigmoid(cv)
        store_groups(mq_ref, rows, groups(s[:, 0:D_MQK // 2] * (DK_M ** -0.5), zero))
        store_groups(mk_ref, rows, groups(s[:, D_MQK // 2:D_MQK], zero))
        yield

        mo_ref[rows, :] = _dot(u, wnat_ref[:, D_MQK:D_MQK + D_MV]).astype(BF16)
        yield

        zt = lax.dot_general(wt_ref[...], u, NT, preferred_element_type=F32)
        mvt_ref[:, rows] = zt[0:D_MV].astype(BF16)
        fvt_ref[:, rows] = zt[D_MV:D_MV + D_FV].astype(BF16)
        g16 = zt[D_MV + D_FV:D_MV + D_FV + N_GATES] + jnp.tile(bias_ref[...], (1, UNIT // LANES))
        yield
        lf = jnp.minimum(g16, 0.0) - jnp.log1p(jnp.exp(-jnp.abs(g16)))
        xs = jnp.concatenate(_split3(lf), axis=0)
        cum = fold(_dot(xs, ucum_ref[...]))
        bch = fold(_dot(xs, ubd_ref[...]))
        tot = fold(_dot(xs, jnp.ones((UNIT, LANES), BF16)))
        c16 = cum + jnp.tile(carry[...], (1, UNIT // LANES))
        carry[...] = carry[...] + tot
        mg_ref[:, rows] = jnp.where(r8 < NH_M, g16[0:2 * NH_M], bch[0:2 * NH_M])
        ys = jnp.concatenate(_split3(-LOG2E * c16), axis=0)
        place = lax.dot_general(ys, e48_ref[...], TN, preferred_element_type=F32)
        yield
        zfq = _dot(u, wnat_ref[:, D_MQK + D_MV:D_MQK + D_MV + D_FV]) * (DH_F ** -0.5 * LOG2E)
        store_groups(fq_ref, rows, groups(zfq, lambda g: ones_hi if g % 2 else ones_lo))
        yield
        zfk = _dot(u, wnat_ref[:, D_MQK + D_MV + D_FV:D_MQK + D_MV + 2 * D_FV])
        store_groups(fk_ref, rows, groups(zfk, lambda g: place[:, LANES * g:LANES * (g + 1)]))
        yield

    _skew([unit(i) for i in range(n_units)], lag=N_FF_CHUNKS + 1, early=N_FF_CHUNKS - 2)


def _const_spec(shape):
    nd = len(shape)
    return pl.BlockSpec(shape, lambda *_: (0,) * nd, pipeline_mode=pl.Buffered(1))


def _mix_in(x, n1, wg, wu, wd, nm, wnat, wt, conv, bias16, ucum, ubd, e48):
    B, S, D = x.shape
    n_t = S // TM_IN
    tok = lambda w: pl.BlockSpec((None, TM_IN, w), lambda b, i: (b, i, 0))
    tokt = lambda r: pl.BlockSpec((None, r, TM_IN), lambda b, i: (b, 0, i))
    consts = (n1, wg, wu, wd, nm, wnat, wt, conv, bias16, ucum, ubd, e48)
    out_shape = (
        jax.ShapeDtypeStruct((B, S, D), F32),
        jax.ShapeDtypeStruct((B, S, NH_M * LANES), BF16),
        jax.ShapeDtypeStruct((B, S, NH_M * LANES), BF16),
        jax.ShapeDtypeStruct((B, S, D_MV), BF16),
        jax.ShapeDtypeStruct((B, D_MV, S), BF16),
        jax.ShapeDtypeStruct((B, 2 * NH_M, S), F32),
        jax.ShapeDtypeStruct((B, S, NH_F * LANES), BF16),
        jax.ShapeDtypeStruct((B, S, NH_F * LANES), BF16),
        jax.ShapeDtypeStruct((B, D_FV, S), BF16),
    )
    out_specs = (tok(D), tok(NH_M * LANES), tok(NH_M * LANES), tok(D_MV), tokt(D_MV), tokt(2 * NH_M),
                 tok(NH_F * LANES), tok(NH_F * LANES), tokt(D_FV))
    return pl.pallas_call(
        _mix_in_kernel,
        grid=(B, n_t),
        in_specs=[tok(D)] + [_const_spec(c.shape) for c in consts],
        out_specs=out_specs,
        out_shape=out_shape,
        scratch_shapes=[pltpu.VMEM((TM_IN + SUBLANES, D_MQK), F32), pltpu.VMEM((N_GATES, LANES), F32)],
        compiler_params=pltpu.CompilerParams(
            dimension_semantics=("arbitrary", "arbitrary"), vmem_limit_bytes=VMEM_LIMIT),
        name="mix_in",
    )(x, *consts)


def _mlstm_kernel(q_ref, k_ref, vt_ref, mg_ref, mo_ref, gain_ref, o_ref, ct_ref):
    S = q_ref.shape[0]
    L = L_M
    ct_ref[...] = jnp.zeros(ct_ref.shape, F32)
    causal = (lax.broadcasted_iota(jnp.int32, (L, L), 0) <= lax.broadcasted_iota(jnp.int32, (L, L), 1))
    r16 = lax.broadcasted_iota(jnp.int32, (BF16_ROWS, L), 0)
    ones_rows = jnp.where(r16 < 3, 1.0, 0.0).astype(BF16)
    ones16 = jnp.ones((BF16_ROWS, L), BF16)

    def head_chunk(cs, h, ms):
        hs = slice(LANES * h, LANES * (h + 1))
        m = ms[h]
        logi = mg_ref[h:h + 1, pl.ds(cs, L)]
        b = mg_ref[NH_M + h:NH_M + h + 1, pl.ds(cs, L)]
        a = logi - b
        gtot = b[:, L - 1:L]
        m_new = jnp.maximum(gtot + m, gtot + jnp.max(a, axis=1, keepdims=True))
        ms[h] = m_new
        ahi, amid, alo = (p.astype(F32) for p in _split3(a))
        at = jnp.where(r16 == 0, ahi, jnp.where(r16 == 1, amid, jnp.where(r16 == 2, alo, 0.0)))
        acol = lax.dot_general(at.astype(BF16), ones_rows, TN, preferred_element_type=F32)
        q = q_ref[pl.ds(cs, L), hs]
        k = k_ref[pl.ds(cs, L), hs]
        st = lax.dot_general(k, q, NT, preferred_element_type=F32)
        ct = ct_ref[h]
        inter = lax.dot_general(ct.astype(BF16), q, NT, preferred_element_type=F32)
        yield
        e = jnp.where(causal, acol, -jnp.inf)
        mu = jnp.maximum(m, jnp.max(e, axis=0, keepdims=True))
        pt = (jnp.exp(e - mu) * st).astype(BF16)
        vt_aug = jnp.concatenate([vt_ref[hs, pl.ds(cs, L)], ones16], axis=0)
        w = jnp.exp(a + gtot - m_new)
        vtw = (vt_aug.astype(F32) * w).astype(BF16)
        yield
        nd = jnp.exp(m - mu) * inter + _dot(vt_aug, pt)
        ct_ref[h] = jnp.exp(gtot + m - m_new) * ct + _dot(vtw, k)
        yield
        rden = 1.0 / jnp.maximum(jnp.abs(nd[DV_M:DV_M + 1]), jnp.exp(-(b + mu)))
        ht = nd[0:DV_M] * rden
        yt = (ht * lax.rsqrt(jnp.mean(ht * ht, axis=0, keepdims=True) + EPS)
              * jnp.tile(gain_ref[hs, :], (1, L // LANES)))
        og = jax.nn.sigmoid(mo_ref[pl.ds(cs, L), hs].astype(F32))
        o_ref[pl.ds(cs, L), hs] = (yt.T * og).astype(o_ref.dtype)
        yield

    def trip(c, ms):
        ms = list(ms)
        starts = [pl.multiple_of((c * MLSTM_CHUNKS + j) * L, L) for j in range(MLSTM_CHUNKS)]
        _skew([head_chunk(cs, h, ms) for cs in starts for h in range(NH_M)], lag=1, early=0)
        return tuple(ms)

    lax.fori_loop(0, S // (L * MLSTM_CHUNKS), trip, tuple(jnp.zeros((1, 1), F32) for _ in range(NH_M)))


def _mlstm(mq, mk, mvt, mg, mo, gain):
    B, S, _ = mq.shape
    tok = pl.BlockSpec((None, S, NH_M * LANES), lambda b: (b, 0, 0))
    return pl.pallas_call(
        _mlstm_kernel,
        grid=(B,),
        in_specs=[tok, tok,
                  pl.BlockSpec((None, D_MV, S), lambda b: (b, 0, 0)),
                  pl.BlockSpec((None, 2 * NH_M, S), lambda b: (b, 0, 0)),
                  pl.BlockSpec((None, S, D_MV), lambda b: (b, 0, 0)),
                  _const_spec(gain.shape)],
        out_specs=pl.BlockSpec((None, S, D_MV), lambda b: (b, 0, 0)),
        out_shape=jax.ShapeDtypeStruct((B, S, D_MV), BF16),
        scratch_shapes=[pltpu.VMEM((NH_M, DV_M + BF16_ROWS, LANES), F32)],
        compiler_params=pltpu.CompilerParams(
            dimension_semantics=("arbitrary",), vmem_limit_bytes=VMEM_LIMIT),
        name="mlstm",
    )(mq, mk, mvt, mg, mo, gain)


def _fox_kernel(q_ref, k_ref, vt_ref, gain_ref, o_ref, st_a, st_b, mb_a, mb_b, m_ref, acc_ref, bias_ref):
    S = q_ref.shape[0]
    T = T_F
    n_q = S // T
    heads = [slice(LANES * j, LANES * (j + 1)) for j in range(2)]
    bufs = ((st_a, mb_a), (st_b, mb_b))
    n_strips = T // MXU_COLS
    parts = [(j, hc) for j in range(2) for hc in range(n_strips)]
    ones16 = jnp.ones((BF16_ROWS, T), BF16)
    for hc in range(n_strips):
        visible = (lax.broadcasted_iota(jnp.int32, (T, MXU_COLS), 0)
                   <= lax.broadcasted_iota(jnp.int32, (T, MXU_COLS), 1) + hc * MXU_COLS)
        bias_ref[hc] = jnp.where(visible, 0.0, NEG)

    def n_keys(diag, hc):
        return (hc + 1) * MXU_COLS if diag else T

    def logits(c, buf, diag, part):
        qi, kj = c
        st_buf, mb_buf = buf
        j, hc = part
        nk = n_keys(diag, hc)
        qs = pl.multiple_of(qi * T + hc * MXU_COLS, MXU_COLS)
        ks = pl.multiple_of(kj * T, T)
        st = lax.dot_general(k_ref[pl.ds(ks, nk), heads[j]], q_ref[pl.ds(qs, MXU_COLS), heads[j]], NT,
                             preferred_element_type=F32)
        if diag:
            st = st + bias_ref[hc, 0:nk, :]
        st_buf[j, hc, 0:nk, :] = st
        mb_buf[j, hc] = jnp.max(st, axis=0, keepdims=True)

    def update(c, buf, first, part):
        qi, kj = c
        st_buf, mb_buf = buf
        j, hc = part
        nk = n_keys(first, hc)
        ks = pl.multiple_of(kj * T, T)
        m_old = m_ref[qi, j, hc]
        m_new = mb_buf[j, hc] if first else jnp.maximum(m_old, mb_buf[j, hc])
        pt = jnp.exp2(st_buf[j, hc, 0:nk, :] - m_new).astype(BF16)
        vt_aug = jnp.concatenate([vt_ref[DH_F * j:DH_F * (j + 1), pl.ds(ks, nk)], ones16[:, 0:nk]], axis=0)
        pv = _dot(vt_aug, pt)
        acc_ref[qi, j, hc] = pv if first else jnp.exp2(m_old - m_new) * acc_ref[qi, j, hc] + pv
        m_ref[qi, j, hc] = m_new

    def step(cur, nxt, b, diag):
        for part in parts:
            if nxt is not None:
                logits(nxt, bufs[1 - b], diag, part)
            update(cur, bufs[b], diag, part)

    def sweep(n, start, succ, diag):
        if n == 0:
            return
        for part in parts:
            logits(start, bufs[0], diag, part)

        def trip(_, c):
            for i in range(FOX_UNROLL):
                cn = succ(c)
                step(c, cn, i % 2, diag)
                c = cn
            return c

        trips = (n - 1) // FOX_UNROLL
        c = lax.fori_loop(0, trips, trip, start)
        rest = n - trips * FOX_UNROLL
        for i in range(rest):
            cn = succ(c) if i < rest - 1 else None
            step(c, cn, i % 2, diag)
            c = cn

    zero = jnp.int32(0)
    sweep(n_q, (zero, zero), lambda c: (c[0] + 1, c[1] + 1), True)

    def below(c):
        wrap = c[1] + 1 >= c[0]
        return jnp.where(wrap, c[0] + 1, c[0]), jnp.where(wrap, 0, c[1] + 1)

    sweep(n_q * (n_q - 1) // 2, (zero + 1, zero), below, False)

    def finish(qi, _):
        ys = []
        for j in range(2):
            acc = jnp.concatenate([acc_ref[qi, j, hc] for hc in range(n_strips)], axis=1)
            o = acc[0:DH_F] * (1.0 / acc[DH_F:DH_F + 1])
            ys.append(o * lax.rsqrt(jnp.mean(o * o, axis=0, keepdims=True) + EPS)
                      * jnp.tile(gain_ref[DH_F * j:DH_F * (j + 1), :], (1, T // LANES)))
        o_ref[pl.ds(pl.multiple_of(qi * T, T), T), :] = jnp.concatenate(ys, axis=0).T.astype(o_ref.dtype)
        return 0

    lax.fori_loop(0, n_q, finish, 0, unroll=FOX_FINISH_UNROLL)


def _fox(fq, fk, fvt, gain):
    B, S, _ = fq.shape
    n_pair = NH_F // 2
    n_strips = T_F // MXU_COLS
    tok = pl.BlockSpec((None, S, 2 * LANES), lambda b, p: (b, 0, p))
    return pl.pallas_call(
        _fox_kernel,
        grid=(B, n_pair),
        in_specs=[tok, tok,
                  pl.BlockSpec((None, 2 * DH_F, S), lambda b, p: (b, p, 0)),
                  pl.BlockSpec((2 * DH_F, LANES), lambda b, p: (p, 0))],
        out_specs=pl.BlockSpec((None, S, 2 * DH_F), lambda b, p: (b, 0, p)),
        out_shape=jax.ShapeDtypeStruct((B, S, D_FV), BF16),
        scratch_shapes=[pltpu.VMEM((2, n_strips, T_F, MXU_COLS), F32), pltpu.VMEM((2, n_strips, T_F, MXU_COLS), F32),
                        pltpu.VMEM((2, n_strips, 1, MXU_COLS), F32), pltpu.VMEM((2, n_strips, 1, MXU_COLS), F32),
                        pltpu.VMEM((S // T_F, 2, n_strips, 1, MXU_COLS), F32),
                        pltpu.VMEM((S // T_F, 2, n_strips, DH_F + BF16_ROWS, MXU_COLS), F32),
                        pltpu.VMEM((n_strips, T_F, MXU_COLS), F32)],
        compiler_params=pltpu.CompilerParams(
            dimension_semantics=("arbitrary", "arbitrary"), vmem_limit_bytes=VMEM_LIMIT),
        name="fox",
    )(fq, fk, fvt, gain)


def _mix_out_kernel(h_ref, ym_ref, yf_ref, p_ref, wom_ref, wof_ref, n2_ref, wg_ref, wu_ref, wd_ref,
                    ng_ref, wpg_ref, wpp_ref, npp_ref, nf_ref, o_ref):
    def unit(i):
        rows = slice(i * UNIT, (i + 1) * UNIT)
        h = h_ref[rows, :] + _dot(ym_ref[rows, :], wom_ref[...]) + _dot(yf_ref[rows, :], wof_ref[...])
        pe = _rms(_dot(p_ref[rows, :].astype(BF16), wpp_ref[...]), npp_ref[...])
        xn = _rms(h, n2_ref[...]).astype(BF16)
        yield
        acc = None
        for c in range(0, D_FF, FF_CHUNK):
            part = _swiglu_chunk(xn, c, wg_ref, wu_ref, wd_ref)
            acc = part if acc is None else acc + part
            yield
        h = h + 0.5 * acc
        xg = _rms(h, ng_ref[...]).astype(BF16)
        yield
        h = h + jax.nn.sigmoid(_dot(xg, wpg_ref[...])) * pe
        o_ref[rows, :] = _rms(h, nf_ref[...])
        yield

    _skew([unit(i) for i in range(h_ref.shape[0] // UNIT)], lag=N_FF_CHUNKS + 1, early=N_FF_CHUNKS - 2)


def _mix_out(h1, ym, yf, p, *consts):
    B, S, D = h1.shape
    tok = lambda w: pl.BlockSpec((None, TM_OUT, w), lambda b, i: (b, i, 0))
    return pl.pallas_call(
        _mix_out_kernel,
        grid=(B, S // TM_OUT),
        in_specs=[tok(D), tok(D_MV), tok(D_FV), tok(p.shape[-1])] + [_const_spec(c.shape) for c in consts],
        out_specs=tok(D),
        out_shape=jax.ShapeDtypeStruct((B, S, D), F32),
        compiler_params=pltpu.CompilerParams(
            dimension_semantics=("arbitrary", "arbitrary"), vmem_limit_bytes=VMEM_LIMIT),
        name="mix_out",
    )(h1, ym, yf, p, *consts)


_O_MQK = 0
_O_MV = _O_MQK + D_MQK
_O_MO = _O_MV + D_MV
_O_MIF = _O_MO + D_MV
_O_FQ = _O_MIF + 2 * NH_M
_O_FK = _O_FQ + D_FV
_O_FV = _O_FK + D_FV
_O_FF = _O_FV + D_FV
_NAT_COLS = np.concatenate([np.arange(_O_MQK, _O_MQK + D_MQK), np.arange(_O_MO, _O_MO + D_MV),
                            np.arange(_O_FQ, _O_FQ + D_FV), np.arange(_O_FK, _O_FK + D_FV)])
_T_COLS = np.concatenate([np.arange(_O_MV, _O_MV + D_MV), np.arange(_O_FV, _O_FV + D_FV),
                          np.arange(_O_MIF, _O_MIF + 2 * NH_M), np.arange(_O_FF, _O_FF + NH_F)])


def _placement():
    e = np.zeros((3 * N_GATES, NH_F * LANES), np.float32)
    for j in range(3):
        for h in range(NH_F):
            e[N_GATES * j + 2 * NH_M + h, LANES * h + (j if h % 2 else DH_F + j)] = 1.0
    return e


def _tri(n, block):
    s = np.arange(n)[:, None]
    t = np.arange(n)[None, :]
    return ((s <= t) & (s // block == t // block)).astype(np.float32)


def _take_cols(w, cols):
    cuts = [0] + [i for i in range(1, len(cols)) if cols[i] != cols[i - 1] + 1] + [len(cols)]
    return jnp.concatenate([w[:, int(cols[a]):int(cols[b - 1]) + 1] for a, b in zip(cuts[:-1], cuts[1:])], axis=1)


def kernel(x, p, ffn1_norm, ffn1_w_gate, ffn1_w_up, ffn1_w_down, mix_norm, w_in, conv_qk, b_mlstm_gates, b_fox_f, mlstm_out_norm, fox_out_norm, w_out, ffn2_norm, ffn2_w_gate, ffn2_w_up, ffn2_w_down, ple_gate_norm, w_ple_gate, w_ple_proj, ple_proj_norm, final_norm):
    depth = w_in.shape[0]
    assert depth == 1, "the final RMSNorm is fused into the (single) layer's last kernel"
    assert x.shape[1] % TM_OUT == 0 and x.shape[1] % (L_M * MLSTM_CHUNKS) == 0 and x.shape[2] == D_MODEL
    assert TM_IN % UNIT == 0 and TM_OUT % UNIT == 0 and UNIT % L_M == 0 and FOX_UNROLL % 2 == 0
    row = lambda v: v.reshape(1, -1).astype(F32)
    col = lambda v: jnp.broadcast_to(v.astype(F32)[:, None], (v.shape[0], LANES))
    ucum = jnp.asarray(_tri(UNIT, UNIT), BF16)
    ubd = jnp.asarray(_tri(UNIT, L_M), BF16)
    e48 = jnp.asarray(_placement(), BF16)
    h = x
    for i in range(depth):
        wnat = _take_cols(w_in[i], _NAT_COLS).astype(BF16)
        wt = _take_cols(w_in[i], _T_COLS).T.astype(BF16)
        conv = conv_qk[i].astype(F32)
        bias16 = col(jnp.concatenate([b_mlstm_gates[i], b_fox_f[i]]))
        h1, mq, mk, mo, mvt, mg, fq, fk, fvt = _mix_in(
            h, row(ffn1_norm[i]), ffn1_w_gate[i].astype(BF16), ffn1_w_up[i].astype(BF16),
            ffn1_w_down[i].astype(BF16), row(mix_norm[i]), wnat, wt, conv, bias16, ucum, ubd, e48)
        ym = _mlstm(mq, mk, mvt, mg, mo, col(mlstm_out_norm[i]))
        yf = _fox(fq, fk, fvt, col(fox_out_norm[i]))
        wo = w_out[i].astype(BF16)
        h = _mix_out(
            h1, ym, yf, p[i], wo[:D_MV], wo[D_MV:], row(ffn2_norm[i]), ffn2_w_gate[i].astype(BF16),
            ffn2_w_up[i].astype(BF16), ffn2_w_down[i].astype(BF16), row(ple_gate_norm[i]),
            w_ple_gate[i].astype(BF16), w_ple_proj[i].astype(BF16), row(ple_proj_norm[i]),
            row(final_norm))
    return h
```

```python
import numpy as np
import jax
import jax.numpy as jnp
from jax import lax
from jax.experimental import pallas as pl
from jax.experimental.pallas import tpu as pltpu

F32 = jnp.float32
BF16 = jnp.bfloat16

EPS = 1e-6
D_MODEL = 1024
D_FF = 2816
NH_M = 4
DV_M = 128
DK_M = 64
CONV_W = 4
NH_F = 8
DH_F = 64
D_MV = NH_M * DV_M
D_FV = NH_F * DH_F
D_MQK = 2 * NH_M * DK_M
N_GATES = 2 * NH_M + NH_F

LANES = 128
SUBLANES = 8
BF16_ROWS = 16
MXU_COLS = 256
TM_IN = 512
TM_OUT = 1024
UNIT = 512
FF_CHUNK = 256
N_FF_CHUNKS = D_FF // FF_CHUNK
L_M = 256
MLSTM_CHUNKS = 4
T_F = 512
LOG2E = 1.4426950408889634
FOX_HEADS = 4
FOX_UNROLL = 8
FOX_FINISH_UNROLL = 4
NEG = -1e30
VMEM_LIMIT = 60 * 1024 * 1024

NT = (((1,), (1,)), ((), ()))
TN = (((0,), (0,)), ((), ()))


def _rms(x, g):
    return x * lax.rsqrt(jnp.mean(x * x, axis=-1, keepdims=True) + EPS) * g


def _dot(a, b):
    return jnp.dot(a, b, preferred_element_type=F32)


def _split3(x):
    hi = x.astype(BF16)
    r = x - hi.astype(F32)
    mid = r.astype(BF16)
    lo = (r - mid.astype(F32)).astype(BF16)
    return hi, mid, lo


def _swiglu_chunk(xn, c, wg_ref, wu_ref, wd_ref):
    a = _dot(xn, wg_ref[:, c:c + FF_CHUNK])
    b = _dot(xn, wu_ref[:, c:c + FF_CHUNK])
    return _dot((a * jax.nn.sigmoid(a) * b).astype(BF16), wd_ref[c:c + FF_CHUNK, :])


def _skew(units, lag, early):
    done = [0] * len(units)
    tick = 0
    while any(d >= 0 for d in done):
        for i, u in enumerate(units):
            due = max(0, i * lag - early) if done[i] == 0 else i * lag + done[i]
            if done[i] >= 0 and tick >= due:
                try:
                    next(u)
                    done[i] += 1
                except StopIteration:
                    done[i] = -1
        tick += 1


def _mix_in_kernel(x_ref, n1_ref, wg_ref, wu_ref, wd_ref, nm_ref, wnat_ref, wt_ref, conv_ref,
                   bias_ref, ucum_ref, ubd_ref, e48_ref,
                   h1_ref, mq_ref, mk_ref, mo_ref, mvt_ref, mg_ref, fq_ref, fk_ref, fvt_ref,
                   cbuf, carry):
    tm = x_ref.shape[0]
    n_units = tm // UNIT

    @pl.when(pl.program_id(1) == 0)
    def _():
        cbuf[0:SUBLANES, :] = jnp.zeros((SUBLANES, cbuf.shape[1]), F32)
        carry[...] = jnp.zeros(carry.shape, F32)

    lane = lax.broadcasted_iota(jnp.int32, (UNIT, LANES), 1)
    low = lane < DK_M
    ones_lo = jnp.where((lane >= DH_F) & (lane < DH_F + 3), 1.0, 0.0)
    ones_hi = jnp.where(lane < 3, 1.0, 0.0)
    r8 = lax.broadcasted_iota(jnp.int32, (2 * NH_M, UNIT), 0)

    def groups(z, fill):
        out = []
        for j in range(z.shape[1] // LANES):
            c = z[:, LANES * j:LANES * (j + 1)]
            out += [jnp.where(low, c, fill(2 * j)), jnp.where(low, fill(2 * j + 1), c)]
        return out

    def store_groups(ref, rows, gs):
        for g, v in enumerate(gs):
            ref[rows, LANES * g:LANES * (g + 1)] = v.astype(ref.dtype)

    def fold(r):
        return r[0:N_GATES] + r[N_GATES:2 * N_GATES] + r[2 * N_GATES:3 * N_GATES]

    zero = lambda j: 0.0

    def unit(i):
        r0 = i * UNIT
        rows = slice(r0, r0 + UNIT)
        x = x_ref[rows, :]
        xn = _rms(x, n1_ref[...]).astype(BF16)
        yield
        acc = None
        for c in range(0, D_FF, FF_CHUNK):
            part = _swiglu_chunk(xn, c, wg_ref, wu_ref, wd_ref)
            acc = part if acc is None else acc + part
            yield
        h1 = x + 0.5 * acc
        h1_ref[rows, :] = h1
        u = _rms(h1, nm_ref[...]).astype(BF16)
        yield

        zqk = _dot(u, wnat_ref[:, 0:D_MQK])
        cbuf[SUBLANES + r0:SUBLANES + r0 + UNIT, :] = zqk
        yield
        cv = conv_ref[CONV_W - 1:CONV_W, :] * zqk
        for j in range(CONV_W - 1):
            top = SUBLANES + r0 - (CONV_W - 1) + j
            cv = cv + conv_ref[j:j + 1, :] * cbuf[top:top + UNIT, :]
        if i == n_units - 1:
            cbuf[0:SUBLANES, :] = cbuf[tm:tm + SUBLANES, :]
        s = cv * jax.nn.sigmoid(cv)
        store_groups(mq_ref, rows, groups(s[:, 0:D_MQK // 2] * (DK_M ** -0.5), zero))
        store_groups(mk_ref, rows, groups(s[:, D_MQK // 2:D_MQK], zero))
        yield

        mo_ref[rows, :] = _dot(u, wnat_ref[:, D_MQK:D_MQK + D_MV]).astype(BF16)
        yield

        zt = lax.dot_general(wt_ref[...], u, NT, preferred_element_type=F32)
        mvt_ref[:, rows] = zt[0:D_MV].astype(BF16)
        fvt_ref[:, rows] = zt[D_MV:D_MV + D_FV].astype(BF16)
        g16 = zt[D_MV + D_FV:D_MV + D_FV + N_GATES] + jnp.tile(bias_ref[...], (1, UNIT // LANES))
        yield
        lf = jnp.minimum(g16, 0.0) - jnp.log1p(jnp.exp(-jnp.abs(g16)))
        xs = jnp.concatenate(_split3(lf), axis=0)
        cum = fold(_dot(xs, ucum_ref[...]))
        bch = fold(_dot(xs, ubd_ref[...]))
        tot = fold(_dot(xs, jnp.ones((UNIT, LANES), BF16)))
        c16 = cum + jnp.tile(carry[...], (1, UNIT // LANES))
        carry[...] = carry[...] + tot
        mg_ref[:, rows] = jnp.where(r8 < NH_M, g16[0:2 * NH_M], bch[0:2 * NH_M])
        ys = jnp.concatenate(_split3(-LOG2E * c16), axis=0)
        place = lax.dot_general(ys, e48_ref[...], TN, preferred_element_type=F32)
        yield
        zfq = _dot(u, wnat_ref[:, D_MQK + D_MV:D_MQK + D_MV + D_FV]) * (DH_F ** -0.5 * LOG2E)
        store_groups(fq_ref, rows, groups(zfq, lambda g: ones_hi if g % 2 else ones_lo))
        yield
        zfk = _dot(u, wnat_ref[:, D_MQK + D_MV + D_FV:D_MQK + D_MV + 2 * D_FV])
        store_groups(fk_ref, rows, groups(zfk, lambda g: place[:, LANES * g:LANES * (g + 1)]))
        yield

    _skew([unit(i) for i in range(n_units)], lag=N_FF_CHUNKS + 1, early=N_FF_CHUNKS - 2)


def _const_spec(shape):
    nd = len(shape)
    return pl.BlockSpec(shape, lambda *_: (0,) * nd, pipeline_mode=pl.Buffered(1))


def _mix_in(x, n1, wg, wu, wd, nm, wnat, wt, conv, bias16, ucum, ubd, e48):
    B, S, D = x.shape
    n_t = S // TM_IN
    tok = lambda w: pl.BlockSpec((None, TM_IN, w), lambda b, i: (b, i, 0))
    tokt = lambda r: pl.BlockSpec((None, r, TM_IN), lambda b, i: (b, 0, i))
    consts = (n1, wg, wu, wd, nm, wnat, wt, conv, bias16, ucum, ubd, e48)
    out_shape = (
        jax.ShapeDtypeStruct((B, S, D), F32),
        jax.ShapeDtypeStruct((B, S, NH_M * LANES), BF16),
        jax.ShapeDtypeStruct((B, S, NH_M * LANES), BF16),
        jax.ShapeDtypeStruct((B, S, D_MV), BF16),
        jax.ShapeDtypeStruct((B, D_MV, S), BF16),
        jax.ShapeDtypeStruct((B, 2 * NH_M, S), F32),
        jax.ShapeDtypeStruct((B, S, NH_F * LANES), BF16),
        jax.ShapeDtypeStruct((B, S, NH_F * LANES), BF16),
        jax.ShapeDtypeStruct((B, D_FV, S), BF16),
    )
    out_specs = (tok(D), tok(NH_M * LANES), tok(NH_M * LANES), tok(D_MV), tokt(D_MV), tokt(2 * NH_M),
                 tok(NH_F * LANES), tok(NH_F * LANES), tokt(D_FV))
    return pl.pallas_call(
        _mix_in_kernel,
        grid=(B, n_t),
        in_specs=[tok(D)] + [_const_spec(c.shape) for c in consts],
        out_specs=out_specs,
        out_shape=out_shape,
        scratch_shapes=[pltpu.VMEM((TM_IN + SUBLANES, D_MQK), F32), pltpu.VMEM((N_GATES, LANES), F32)],
        compiler_params=pltpu.CompilerParams(
            dimension_semantics=("arbitrary", "arbitrary"), vmem_limit_bytes=VMEM_LIMIT),
        name="mix_in",
    )(x, *consts)


def _mlstm_kernel(q_ref, k_ref, vt_ref, mg_ref, mo_ref, gain_ref, o_ref, ct_ref):
    S = q_ref.shape[0]
    L = L_M
    ct_ref[...] = jnp.zeros(ct_ref.shape, F32)
    causal = (lax.broadcasted_iota(jnp.int32, (L, L), 0) <= lax.broadcasted_iota(jnp.int32, (L, L), 1))
    r16 = lax.broadcasted_iota(jnp.int32, (BF16_ROWS, L), 0)
    ones_rows = jnp.where(r16 < 3, 1.0, 0.0).astype(BF16)
    ones16 = jnp.ones((BF16_ROWS, L), BF16)

    def head_chunk(cs, h, ms):
        hs = slice(LANES * h, LANES * (h + 1))
        m = ms[h]
        logi = mg_ref[h:h + 1, pl.ds(cs, L)]
        b = mg_ref[NH_M + h:NH_M + h + 1, pl.ds(cs, L)]
        a = logi - b
        gtot = b[:, L - 1:L]
        m_new = jnp.maximum(gtot + m, gtot + jnp.max(a, axis=1, keepdims=True))
        ms[h] = m_new
        ahi, amid, alo = (p.astype(F32) for p in _split3(a))
        at = jnp.where(r16 == 0, ahi, jnp.where(r16 == 1, amid, jnp.where(r16 == 2, alo, 0.0)))
        acol = lax.dot_general(at.astype(BF16), ones_rows, TN, preferred_element_type=F32)
        q = q_ref[pl.ds(cs, L), hs]
        k = k_ref[pl.ds(cs, L), hs]
        st = lax.dot_general(k, q, NT, preferred_element_type=F32)
        ct = ct_ref[h]
        inter = lax.dot_general(ct.astype(BF16), q, NT, preferred_element_type=F32)
        yield
        e = jnp.where(causal, acol, -jnp.inf)
        mu = jnp.maximum(m, jnp.max(e, axis=0, keepdims=True))
        pt = (jnp.exp(e - mu) * st).astype(BF16)
        vt_aug = jnp.concatenate([vt_ref[hs, pl.ds(cs, L)], ones16], axis=0)
        w = jnp.exp(a + gtot - m_new)
        vtw = (vt_aug.astype(F32) * w).astype(BF16)
        yield
        nd = jnp.exp(m - mu) * inter + _dot(vt_aug, pt)
        ct_ref[h] = jnp.exp(gtot + m - m_new) * ct + _dot(vtw, k)
        yield
        rden = 1.0 / jnp.maximum(jnp.abs(nd[DV_M:DV_M + 1]), jnp.exp(-(b + mu)))
        ht = nd[0:DV_M] * rden
        yt = (ht * lax.rsqrt(jnp.mean(ht * ht, axis=0, keepdims=True) + EPS)
              * jnp.tile(gain_ref[hs, :], (1, L // LANES)))
        og = jax.nn.sigmoid(mo_ref[pl.ds(cs, L), hs].astype(F32))
        o_ref[pl.ds(cs, L), hs] = (yt.T * og).astype(o_ref.dtype)
        yield

    def trip(c, ms):
        ms = list(ms)
        starts = [pl.multiple_of((c * MLSTM_CHUNKS + j) * L, L) for j in range(MLSTM_CHUNKS)]
        _skew([head_chunk(cs, h, ms) for cs in starts for h in range(NH_M)], lag=1, early=0)
        return tuple(ms)

    lax.fori_loop(0, S // (L * MLSTM_CHUNKS), trip, tuple(jnp.zeros((1, 1), F32) for _ in range(NH_M)))


def _mlstm(mq, mk, mvt, mg, mo, gain):
    B, S, _ = mq.shape
    tok = pl.BlockSpec((None, S, NH_M * LANES), lambda b: (b, 0, 0))
    return pl.pallas_call(
        _mlstm_kernel,
        grid=(B,),
        in_specs=[tok, tok,
                  pl.BlockSpec((None, D_MV, S), lambda b: (b, 0, 0)),
                  pl.BlockSpec((None, 2 * NH_M, S), lambda b: (b, 0, 0)),
                  pl.BlockSpec((None, S, D_MV), lambda b: (b, 0, 0)),
                  _const_spec(gain.shape)],
        out_specs=pl.BlockSpec((None, S, D_MV), lambda b: (b, 0, 0)),
        out_shape=jax.ShapeDtypeStruct((B, S, D_MV), BF16),
        scratch_shapes=[pltpu.VMEM((NH_M, DV_M + BF16_ROWS, LANES), F32)],
        compiler_params=pltpu.CompilerParams(
            dimension_semantics=("arbitrary",), vmem_limit_bytes=VMEM_LIMIT),
        name="mlstm",
    )(mq, mk, mvt, mg, mo, gain)


def _fox_kernel(q_ref, k_ref, vt_ref, gain_ref, o_ref, st_a, st_b, mb_a, mb_b, m_ref, acc_ref, bias_ref):
    S = q_ref.shape[0]
    T = T_F
    n_q = S // T
    heads = [slice(LANES * j, LANES * (j + 1)) for j in range(FOX_HEADS)]
    bufs = ((st_a, mb_a), (st_b, mb_b))
    n_strips = T // MXU_COLS
    parts = [(j, hc) for j in range(FOX_HEADS) for hc in range(n_strips)]
    ones16 = jnp.ones((BF16_ROWS, T), BF16)
    for hc in range(n_strips):
        visible = (lax.broadcasted_iota(jnp.int32, (T, MXU_COLS), 0)
                   <= lax.broadcasted_iota(jnp.int32, (T, MXU_COLS), 1) + hc * MXU_COLS)
        bias_ref[hc] = jnp.where(visible, 0.0, NEG)

    def n_keys(diag, hc):
        return (hc + 1) * MXU_COLS if diag else T

    def logits(c, buf, diag, part):
        qi, kj = c
        st_buf, mb_buf = buf
        j, hc = part
        nk = n_keys(diag, hc)
        qs = pl.multiple_of(qi * T + hc * MXU_COLS, MXU_COLS)
        ks = pl.multiple_of(kj * T, T)
        st = lax.dot_general(k_ref[pl.ds(ks, nk), heads[j]], q_ref[pl.ds(qs, MXU_COLS), heads[j]], NT,
                             preferred_element_type=F32)
        if diag:
            st = st + bias_ref[hc, 0:nk, :]
        st_buf[j, hc, 0:nk, :] = st
        mb_buf[j, hc] = jnp.max(st, axis=0, keepdims=True)

    def update(c, buf, first, part):
        qi, kj = c
        st_buf, mb_buf = buf
        j, hc = part
        nk = n_keys(first, hc)
        ks = pl.multiple_of(kj * T, T)
        m_old = m_ref[qi, j, hc]
        m_new = mb_buf[j, hc] if first else jnp.maximum(m_old, mb_buf[j, hc])
        pt = jnp.exp2(st_buf[j, hc, 0:nk, :] - m_new).astype(BF16)
        vt_aug = jnp.concatenate([vt_ref[DH_F * j:DH_F * (j + 1), pl.ds(ks, nk)], ones16[:, 0:nk]], axis=0)
        pv = _dot(vt_aug, pt)
        acc_ref[qi, j, hc] = pv if first else jnp.exp2(m_old - m_new) * acc_ref[qi, j, hc] + pv
        m_ref[qi, j, hc] = m_new

    def step(cur, nxt, b, diag):
        for part in parts:
            if nxt is not None:
                logits(nxt, bufs[1 - b], diag, part)
            update(cur, bufs[b], diag, part)

    def sweep(n, start, succ, diag):
        if n == 0:
            return
        for part in parts:
            logits(start, bufs[0], diag, part)

        def trip(_, c):
            for i in range(FOX_UNROLL):
                cn = succ(c)
                step(c, cn, i % 2, diag)
                c = cn
            return c

        trips = (n - 1) // FOX_UNROLL
        c = lax.fori_loop(0, trips, trip, start)
        rest = n - trips * FOX_UNROLL
        for i in range(rest):
            cn = succ(c) if i < rest - 1 else None
            step(c, cn, i % 2, diag)
            c = cn

    zero = jnp.int32(0)
    sweep(n_q, (zero, zero), lambda c: (c[0] + 1, c[1] + 1), True)

    def below(c):
        wrap = c[1] + 1 >= c[0]
        return jnp.where(wrap, c[0] + 1, c[0]), jnp.where(wrap, 0, c[1] + 1)

    sweep(n_q * (n_q - 1) // 2, (zero + 1, zero), below, False)

    def finish(qi, _):
        ys = []
        for j in range(FOX_HEADS):
            acc = jnp.concatenate([acc_ref[qi, j, hc] for hc in range(n_strips)], axis=1)
            o = acc[0:DH_F] * (1.0 / acc[DH_F:DH_F + 1])
            ys.append(o * lax.rsqrt(jnp.mean(o * o, axis=0, keepdims=True) + EPS)
                      * jnp.tile(gain_ref[DH_F * j:DH_F * (j + 1), :], (1, T // LANES)))
        o_ref[pl.ds(pl.multiple_of(qi * T, T), T), :] = jnp.concatenate(ys, axis=0).T.astype(o_ref.dtype)
        return 0

    lax.fori_loop(0, n_q, finish, 0, unroll=FOX_FINISH_UNROLL)


def _fox(fq, fk, fvt, gain):
    B, S, _ = fq.shape
    n_pair = NH_F // FOX_HEADS
    n_strips = T_F // MXU_COLS
    tok = pl.BlockSpec((None, S, FOX_HEADS * LANES), lambda b, p: (b, 0, p))
    return pl.pallas_call(
        _fox_kernel,
        grid=(B, n_pair),
        in_specs=[tok, tok,
                  pl.BlockSpec((None, FOX_HEADS * DH_F, S), lambda b, p: (b, p, 0)),
                  pl.BlockSpec((FOX_HEADS * DH_F, LANES), lambda b, p: (p, 0))],
        out_specs=pl.BlockSpec((None, S, FOX_HEADS * DH_F), lambda b, p: (b, 0, p)),
        out_shape=jax.ShapeDtypeStruct((B, S, D_FV), BF16),
        scratch_shapes=[pltpu.VMEM((FOX_HEADS, n_strips, T_F, MXU_COLS), F32), pltpu.VMEM((FOX_HEADS, n_strips, T_F, MXU_COLS), F32),
                        pltpu.VMEM((FOX_HEADS, n_strips, 1, MXU_COLS), F32), pltpu.VMEM((FOX_HEADS, n_strips, 1, MXU_COLS), F32),
                        pltpu.VMEM((S // T_F, FOX_HEADS, n_strips, 1, MXU_COLS), F32),
                        pltpu.VMEM((S // T_F, FOX_HEADS, n_strips, DH_F + BF16_ROWS, MXU_COLS), F32),
                        pltpu.VMEM((n_strips, T_F, MXU_COLS), F32)],
        compiler_params=pltpu.CompilerParams(
            dimension_semantics=("arbitrary", "arbitrary"), vmem_limit_bytes=VMEM_LIMIT),
        name="fox",
    )(fq, fk, fvt, gain)


def _mix_out_kernel(h_ref, ym_ref, yf_ref, p_ref, wom_ref, wof_ref, n2_ref, wg_ref, wu_ref, wd_ref,
                    ng_ref, wpg_ref, wpp_ref, npp_ref, nf_ref, o_ref):
    def unit(i):
        rows = slice(i * UNIT, (i + 1) * UNIT)
        h = h_ref[rows, :] + _dot(ym_ref[rows, :], wom_ref[...]) + _dot(yf_ref[rows, :], wof_ref[...])
        pe = _rms(_dot(p_ref[rows, :].astype(BF16), wpp_ref[...]), npp_ref[...])
        xn = _rms(h, n2_ref[...]).astype(BF16)
        yield
        acc = None
        for c in range(0, D_FF, FF_CHUNK):
            part = _swiglu_chunk(xn, c, wg_ref, wu_ref, wd_ref)
            acc = part if acc is None else acc + part
            yield
        h = h + 0.5 * acc
        xg = _rms(h, ng_ref[...]).astype(BF16)
        yield
        h = h + jax.nn.sigmoid(_dot(xg, wpg_ref[...])) * pe
        o_ref[rows, :] = _rms(h, nf_ref[...])
        yield

    _skew([unit(i) for i in range(h_ref.shape[0] // UNIT)], lag=N_FF_CHUNKS + 1, early=N_FF_CHUNKS - 2)


def _mix_out(h1, ym, yf, p, *consts):
    B, S, D = h1.shape
    tok = lambda w: pl.BlockSpec((None, TM_OUT, w), lambda b, i: (b, i, 0))
    return pl.pallas_call(
        _mix_out_kernel,
        grid=(B, S // TM_OUT),
        in_specs=[tok(D), tok(D_MV), tok(D_FV), tok(p.shape[-1])] + [_const_spec(c.shape) for c in consts],
        out_specs=tok(D),
        out_shape=jax.ShapeDtypeStruct((B, S, D), F32),
        compiler_params=pltpu.CompilerParams(
            dimension_semantics=("arbitrary", "arbitrary"), vmem_limit_bytes=VMEM_LIMIT),
        name="mix_out",
    )(h1, ym, yf, p, *consts)


_O_MQK = 0
_O_MV = _O_MQK + D_MQK
_O_MO = _O_MV + D_MV
_O_MIF = _O_MO + D_MV
_O_FQ = _O_MIF + 2 * NH_M
_O_FK = _O_FQ + D_FV
_O_FV = _O_FK + D_FV
_O_FF = _O_FV + D_FV
_NAT_COLS = np.concatenate([np.arange(_O_MQK, _O_MQK + D_MQK), np.arange(_O_MO, _O_MO + D_MV),
                            np.arange(_O_FQ, _O_FQ + D_FV), np.arange(_O_FK, _O_FK + D_FV)])
_T_COLS = np.concatenate([np.arange(_O_MV, _O_MV + D_MV), np.arange(_O_FV, _O_FV + D_FV),
                          np.arange(_O_MIF, _O_MIF + 2 * NH_M), np.arange(_O_FF, _O_FF + NH_F)])


def _placement():
    e = np.zeros((3 * N_GATES, NH_F * LANES), np.float32)
    for j in range(3):
        for h in range(NH_F):
            e[N_GATES * j + 2 * NH_M + h, LANES * h + (j if h % 2 else DH_F + j)] = 1.0
    return e


def _tri(n, block):
    s = np.arange(n)[:, None]
    t = np.arange(n)[None, :]
    return ((s <= t) & (s // block == t // block)).astype(np.float32)


def _take_cols(w, cols):
    cuts = [0] + [i for i in range(1, len(cols)) if cols[i] != cols[i - 1] + 1] + [len(cols)]
    return jnp.concatenate([w[:, int(cols[a]):int(cols[b - 1]) + 1] for a, b in zip(cuts[:-1], cuts[1:])], axis=1)


def kernel(x, p, ffn1_norm, ffn1_w_gate, ffn1_w_up, ffn1_w_down, mix_norm, w_in, conv_qk, b_mlstm_gates, b_fox_f, mlstm_out_norm, fox_out_norm, w_out, ffn2_norm, ffn2_w_gate, ffn2_w_up, ffn2_w_down, ple_gate_norm, w_ple_gate, w_ple_proj, ple_proj_norm, final_norm):
    depth = w_in.shape[0]
    assert depth == 1, "the final RMSNorm is fused into the (single) layer's last kernel"
    assert x.shape[1] % TM_OUT == 0 and x.shape[1] % (L_M * MLSTM_CHUNKS) == 0 and x.shape[2] == D_MODEL
    assert TM_IN % UNIT == 0 and TM_OUT % UNIT == 0 and UNIT % L_M == 0 and FOX_UNROLL % 2 == 0
    row = lambda v: v.reshape(1, -1).astype(F32)
    col = lambda v: jnp.broadcast_to(v.astype(F32)[:, None], (v.shape[0], LANES))
    ucum = jnp.asarray(_tri(UNIT, UNIT), BF16)
    ubd = jnp.asarray(_tri(UNIT, L_M), BF16)
    e48 = jnp.asarray(_placement(), BF16)
    h = x
    for i in range(depth):
        wnat = _take_cols(w_in[i], _NAT_COLS).astype(BF16)
        wt = _take_cols(w_in[i], _T_COLS).T.astype(BF16)
        conv = conv_qk[i].astype(F32)
        bias16 = col(jnp.concatenate([b_mlstm_gates[i], b_fox_f[i]]))
        h1, mq, mk, mo, mvt, mg, fq, fk, fvt = _mix_in(
            h, row(ffn1_norm[i]), ffn1_w_gate[i].astype(BF16), ffn1_w_up[i].astype(BF16),
            ffn1_w_down[i].astype(BF16), row(mix_norm[i]), wnat, wt, conv, bias16, ucum, ubd, e48)
        ym = _mlstm(mq, mk, mvt, mg, mo, col(mlstm_out_norm[i]))
        yf = _fox(fq, fk, fvt, col(fox_out_norm[i]))
        wo = w_out[i].astype(BF16)
        h = _mix_out(
            h1, ym, yf, p[i], wo[:D_MV], wo[D_MV:], row(ffn2_norm[i]), ffn2_w_gate[i].astype(BF16),
            ffn2_w_up[i].astype(BF16), ffn2_w_down[i].astype(BF16), row(ple_gate_norm[i]),
            w_ple_gate[i].astype(BF16), w_ple_proj[i].astype(BF16), row(ple_proj_norm[i]),
            row(final_norm))
    return h
```

```python
import numpy as np
import jax
import jax.numpy as jnp
from jax import lax
from jax.experimental import pallas as pl
from jax.experimental.pallas import tpu as pltpu

F32 = jnp.float32
BF16 = jnp.bfloat16

EPS = 1e-6
D_MODEL = 1024
D_FF = 2816
NH_M = 4
DV_M = 128
DK_M = 64
CONV_W = 4
NH_F = 8
DH_F = 64
D_MV = NH_M * DV_M
D_FV = NH_F * DH_F
D_MQK = 2 * NH_M * DK_M
N_GATES = 2 * NH_M + NH_F

LANES = 128
SUBLANES = 8
BF16_ROWS = 16
MXU_COLS = 256
TM_IN = 512
TM_OUT = 1024
UNIT = 512
FF_CHUNK = 256
N_FF_CHUNKS = D_FF // FF_CHUNK
L_M = 256
MLSTM_CHUNKS = 8
T_F = 512
LOG2E = 1.4426950408889634
FOX_HEADS = 4
FOX_UNROLL = 8
FOX_FINISH_UNROLL = 4
NEG = -1e30
VMEM_LIMIT = 60 * 1024 * 1024

NT = (((1,), (1,)), ((), ()))
TN = (((0,), (0,)), ((), ()))


def _unit_rms(x):
    return x * lax.rsqrt(jnp.mean(x * x, axis=-1, keepdims=True) + EPS)


def _rms(x, g):
    return _unit_rms(x) * g


def _dot(a, b):
    return jnp.dot(a, b, preferred_element_type=F32)


def _split3(x):
    hi = x.astype(BF16)
    r = x - hi.astype(F32)
    mid = r.astype(BF16)
    lo = (r - mid.astype(F32)).astype(BF16)
    return hi, mid, lo


def _swiglu_chunk(xn, c, wg_ref, wu_ref, wd_ref):
    a = _dot(xn, wg_ref[:, c:c + FF_CHUNK])
    b = _dot(xn, wu_ref[:, c:c + FF_CHUNK])
    return _dot((a * jax.nn.sigmoid(a) * b).astype(BF16), wd_ref[c:c + FF_CHUNK, :])


def _skew(units, lag, early):
    done = [0] * len(units)
    tick = 0
    while any(d >= 0 for d in done):
        for i, u in enumerate(units):
            due = max(0, i * lag - early) if done[i] == 0 else i * lag + done[i]
            if done[i] >= 0 and tick >= due:
                try:
                    next(u)
                    done[i] += 1
                except StopIteration:
                    done[i] = -1
        tick += 1


def _mix_in_kernel(x_ref, wg_ref, wu_ref, wd_ref, wnat_ref, wt_ref, conv_ref,
                   bias_ref, ucum_ref, ubd_ref, e48_ref,
                   h1_ref, mq_ref, mk_ref, mo_ref, mvt_ref, mg_ref, fq_ref, fk_ref, fvt_ref,
                   cbuf, carry):
    tm = x_ref.shape[0]
    n_units = tm // UNIT

    @pl.when(pl.program_id(1) == 0)
    def _():
        cbuf[0:SUBLANES, :] = jnp.zeros((SUBLANES, cbuf.shape[1]), F32)
        carry[...] = jnp.zeros(carry.shape, F32)

    lane = lax.broadcasted_iota(jnp.int32, (UNIT, LANES), 1)
    low = lane < DK_M
    ones_lo = jnp.where((lane >= DH_F) & (lane < DH_F + 3), 1.0, 0.0)
    ones_hi = jnp.where(lane < 3, 1.0, 0.0)
    r8 = lax.broadcasted_iota(jnp.int32, (2 * NH_M, UNIT), 0)

    def groups(z, fill):
        out = []
        for j in range(z.shape[1] // LANES):
            c = z[:, LANES * j:LANES * (j + 1)]
            out += [jnp.where(low, c, fill(2 * j)), jnp.where(low, fill(2 * j + 1), c)]
        return out

    def store_groups(ref, rows, gs):
        for g, v in enumerate(gs):
            ref[rows, LANES * g:LANES * (g + 1)] = v.astype(ref.dtype)

    def fold(r):
        return r[0:N_GATES] + r[N_GATES:2 * N_GATES] + r[2 * N_GATES:3 * N_GATES]

    zero = lambda j: 0.0

    def unit(i):
        r0 = i * UNIT
        rows = slice(r0, r0 + UNIT)
        x = x_ref[rows, :]
        xn = _unit_rms(x).astype(BF16)
        yield
        acc = None
        for c in range(0, D_FF, FF_CHUNK):
            part = _swiglu_chunk(xn, c, wg_ref, wu_ref, wd_ref)
            acc = part if acc is None else acc + part
            yield
        h1 = x + acc
        h1_ref[rows, :] = h1
        u = _unit_rms(h1).astype(BF16)
        yield

        zqk = _dot(u, wnat_ref[:, 0:D_MQK])
        cbuf[SUBLANES + r0:SUBLANES + r0 + UNIT, :] = zqk
        yield
        cv = conv_ref[CONV_W - 1:CONV_W, :] * zqk
        for j in range(CONV_W - 1):
            top = SUBLANES + r0 - (CONV_W - 1) + j
            cv = cv + conv_ref[j:j + 1, :] * cbuf[top:top + UNIT, :]
        if i == n_units - 1:
            cbuf[0:SUBLANES, :] = cbuf[tm:tm + SUBLANES, :]
        s = cv * jax.nn.sigmoid(cv)
        store_groups(mq_ref, rows, groups(s[:, 0:D_MQK // 2] * (DK_M ** -0.5), zero))
        store_groups(mk_ref, rows, groups(s[:, D_MQK // 2:D_MQK], zero))
        yield

        mo_ref[rows, :] = _dot(u, wnat_ref[:, D_MQK:D_MQK + D_MV]).astype(BF16)
        yield

        zt = lax.dot_general(wt_ref[...], u, NT, preferred_element_type=F32)
        mvt_ref[:, rows] = zt[0:D_MV].astype(BF16)
        fvt_ref[:, rows] = zt[D_MV:D_MV + D_FV].astype(BF16)
        g16 = zt[D_MV + D_FV:D_MV + D_FV + N_GATES] + jnp.tile(bias_ref[...], (1, UNIT // LANES))
        yield
        lf = jnp.minimum(g16, 0.0) - jnp.log1p(jnp.exp(-jnp.abs(g16)))
        xs = jnp.concatenate(_split3(lf), axis=0)
        cum = fold(_dot(xs, ucum_ref[...]))
        bch = fold(_dot(xs, ubd_ref[...]))
        tot = fold(_dot(xs, jnp.ones((UNIT, LANES), BF16)))
        c16 = cum + jnp.tile(carry[...], (1, UNIT // LANES))
        carry[...] = carry[...] + tot
        mg_ref[:, rows] = jnp.where(r8 < NH_M, g16[0:2 * NH_M], bch[0:2 * NH_M])
        ys = jnp.concatenate(_split3(-LOG2E * c16), axis=0)
        place = lax.dot_general(ys, e48_ref[...], TN, preferred_element_type=F32)
        yield
        zfq = _dot(u, wnat_ref[:, D_MQK + D_MV:D_MQK + D_MV + D_FV]) * (DH_F ** -0.5 * LOG2E)
        store_groups(fq_ref, rows, groups(zfq, lambda g: ones_hi if g % 2 else ones_lo))
        yield
        zfk = _dot(u, wnat_ref[:, D_MQK + D_MV + D_FV:D_MQK + D_MV + 2 * D_FV])
        store_groups(fk_ref, rows, groups(zfk, lambda g: place[:, LANES * g:LANES * (g + 1)]))
        yield

    _skew([unit(i) for i in range(n_units)], lag=N_FF_CHUNKS + 1, early=N_FF_CHUNKS - 2)


def _const_spec(shape):
    nd = len(shape)
    return pl.BlockSpec(shape, lambda *_: (0,) * nd, pipeline_mode=pl.Buffered(1))


def _mix_in(x, wg, wu, wd, wnat, wt, conv, bias16, ucum, ubd, e48):
    B, S, D = x.shape
    n_t = S // TM_IN
    tok = lambda w: pl.BlockSpec((None, TM_IN, w), lambda b, i: (b, i, 0))
    tokt = lambda r: pl.BlockSpec((None, r, TM_IN), lambda b, i: (b, 0, i))
    consts = (wg, wu, wd, wnat, wt, conv, bias16, ucum, ubd, e48)
    out_shape = (
        jax.ShapeDtypeStruct((B, S, D), F32),
        jax.ShapeDtypeStruct((B, S, NH_M * LANES), BF16),
        jax.ShapeDtypeStruct((B, S, NH_M * LANES), BF16),
        jax.ShapeDtypeStruct((B, S, D_MV), BF16),
        jax.ShapeDtypeStruct((B, D_MV, S), BF16),
        jax.ShapeDtypeStruct((B, 2 * NH_M, S), F32),
        jax.ShapeDtypeStruct((B, S, NH_F * LANES), BF16),
        jax.ShapeDtypeStruct((B, S, NH_F * LANES), BF16),
        jax.ShapeDtypeStruct((B, D_FV, S), BF16),
    )
    out_specs = (tok(D), tok(NH_M * LANES), tok(NH_M * LANES), tok(D_MV), tokt(D_MV), tokt(2 * NH_M),
                 tok(NH_F * LANES), tok(NH_F * LANES), tokt(D_FV))
    return pl.pallas_call(
        _mix_in_kernel,
        grid=(B, n_t),
        in_specs=[tok(D)] + [_const_spec(c.shape) for c in consts],
        out_specs=out_specs,
        out_shape=out_shape,
        scratch_shapes=[pltpu.VMEM((TM_IN + SUBLANES, D_MQK), F32), pltpu.VMEM((N_GATES, LANES), F32)],
        compiler_params=pltpu.CompilerParams(
            dimension_semantics=("arbitrary", "arbitrary"), vmem_limit_bytes=VMEM_LIMIT),
        name="mix_in",
    )(x, *consts)


def _mlstm_kernel(q_ref, k_ref, vt_ref, mg_ref, mo_ref, gain_ref, o_ref, ct_ref):
    S = q_ref.shape[0]
    L = L_M
    ct_ref[...] = jnp.zeros(ct_ref.shape, F32)
    causal = (lax.broadcasted_iota(jnp.int32, (L, L), 0) <= lax.broadcasted_iota(jnp.int32, (L, L), 1))
    r16 = lax.broadcasted_iota(jnp.int32, (BF16_ROWS, L), 0)
    ones_rows = jnp.where(r16 < 3, 1.0, 0.0).astype(BF16)
    ones16 = jnp.ones((BF16_ROWS, L), BF16)

    def head_chunk(cs, h, ms):
        hs = slice(LANES * h, LANES * (h + 1))
        m = ms[h]
        logi = LOG2E * mg_ref[h:h + 1, pl.ds(cs, L)]
        b = LOG2E * mg_ref[NH_M + h:NH_M + h + 1, pl.ds(cs, L)]
        a = logi - b
        gtot = b[:, L - 1:L]
        m_new = jnp.maximum(gtot + m, gtot + jnp.max(a, axis=1, keepdims=True))
        ms[h] = m_new
        ahi, amid, alo = (p.astype(F32) for p in _split3(a))
        at = jnp.where(r16 == 0, ahi, jnp.where(r16 == 1, amid, jnp.where(r16 == 2, alo, 0.0)))
        acol = lax.dot_general(at.astype(BF16), ones_rows, TN, preferred_element_type=F32)
        q = q_ref[pl.ds(cs, L), hs]
        k = k_ref[pl.ds(cs, L), hs]
        st = lax.dot_general(k, q, NT, preferred_element_type=F32)
        ct = ct_ref[h]
        inter = lax.dot_general(ct.astype(BF16), q, NT, preferred_element_type=F32)
        yield
        e = jnp.where(causal, acol, -jnp.inf)
        mu = jnp.maximum(m, jnp.max(e, axis=0, keepdims=True))
        pt = (jnp.exp2(e - mu) * st).astype(BF16)
        vt_aug = jnp.concatenate([vt_ref[hs, pl.ds(cs, L)], ones16], axis=0)
        w = jnp.exp2(a + gtot - m_new)
        vtw = vt_aug * w.astype(BF16)
        yield
        nd = jnp.exp2(m - mu) * inter + _dot(vt_aug, pt)
        ct_ref[h] = jnp.exp2(gtot + m - m_new) * ct + _dot(vtw, k)
        yield
        rden = 1.0 / jnp.maximum(jnp.abs(nd[DV_M:DV_M + 1]), jnp.exp2(-(b + mu)))
        ht = nd[0:DV_M] * rden
        yt = (ht * lax.rsqrt(jnp.mean(ht * ht, axis=0, keepdims=True) + EPS)
              * jnp.tile(gain_ref[hs, :], (1, L // LANES)))
        og = jax.nn.sigmoid(mo_ref[pl.ds(cs, L), hs].astype(F32))
        o_ref[pl.ds(cs, L), hs] = (yt.T * og).astype(o_ref.dtype)
        yield

    def trip(c, ms):
        ms = list(ms)
        starts = [pl.multiple_of((c * MLSTM_CHUNKS + j) * L, L) for j in range(MLSTM_CHUNKS)]
        _skew([head_chunk(cs, h, ms) for cs in starts for h in range(NH_M)], lag=1, early=0)
        return tuple(ms)

    lax.fori_loop(0, S // (L * MLSTM_CHUNKS), trip, tuple(jnp.zeros((1, 1), F32) for _ in range(NH_M)))


def _mlstm(mq, mk, mvt, mg, mo, gain):
    B, S, _ = mq.shape
    tok = pl.BlockSpec((None, S, NH_M * LANES), lambda b: (b, 0, 0))
    return pl.pallas_call(
        _mlstm_kernel,
        grid=(B,),
        in_specs=[tok, tok,
                  pl.BlockSpec((None, D_MV, S), lambda b: (b, 0, 0)),
                  pl.BlockSpec((None, 2 * NH_M, S), lambda b: (b, 0, 0)),
                  pl.BlockSpec((None, S, D_MV), lambda b: (b, 0, 0)),
                  _const_spec(gain.shape)],
        out_specs=pl.BlockSpec((None, S, D_MV), lambda b: (b, 0, 0)),
        out_shape=jax.ShapeDtypeStruct((B, S, D_MV), BF16),
        scratch_shapes=[pltpu.VMEM((NH_M, DV_M + BF16_ROWS, LANES), F32)],
        compiler_params=pltpu.CompilerParams(
            dimension_semantics=("arbitrary",), vmem_limit_bytes=VMEM_LIMIT),
        name="mlstm",
    )(mq, mk, mvt, mg, mo, gain)


def _fox_kernel(q_ref, k_ref, vt_ref, gain_ref, o_ref, st_a, st_b, mb_a, mb_b, m_ref, acc_ref, bias_ref):
    S = q_ref.shape[0]
    T = T_F
    n_q = S // T
    heads = [slice(LANES * j, LANES * (j + 1)) for j in range(FOX_HEADS)]
    bufs = ((st_a, mb_a), (st_b, mb_b))
    n_strips = T // MXU_COLS
    parts = [(j, hc) for j in range(FOX_HEADS) for hc in range(n_strips)]
    ones16 = jnp.ones((BF16_ROWS, T), BF16)
    for hc in range(n_strips):
        visible = (lax.broadcasted_iota(jnp.int32, (T, MXU_COLS), 0)
                   <= lax.broadcasted_iota(jnp.int32, (T, MXU_COLS), 1) + hc * MXU_COLS)
        bias_ref[hc] = jnp.where(visible, 0.0, NEG)

    def n_keys(diag, hc):
        return (hc + 1) * MXU_COLS if diag else T

    def logits(c, buf, diag, part):
        qi, kj = c
        st_buf, mb_buf = buf
        j, hc = part
        nk = n_keys(diag, hc)
        qs = pl.multiple_of(qi * T + hc * MXU_COLS, MXU_COLS)
        ks = pl.multiple_of(kj * T, T)
        st = lax.dot_general(k_ref[pl.ds(ks, nk), heads[j]], q_ref[pl.ds(qs, MXU_COLS), heads[j]], NT,
                             preferred_element_type=F32)
        if diag:
            st = st + bias_ref[hc, 0:nk, :]
        st_buf[j, hc, 0:nk, :] = st
        mb_buf[j, hc] = jnp.max(st, axis=0, keepdims=True)

    def update(c, buf, first, part):
        qi, kj = c
        st_buf, mb_buf = buf
        j, hc = part
        nk = n_keys(first, hc)
        ks = pl.multiple_of(kj * T, T)
        m_old = m_ref[qi, j, hc]
        m_new = mb_buf[j, hc] if first else jnp.maximum(m_old, mb_buf[j, hc])
        pt = jnp.exp2(st_buf[j, hc, 0:nk, :] - m_new).astype(BF16)
        vt_aug = jnp.concatenate([vt_ref[DH_F * j:DH_F * (j + 1), pl.ds(ks, nk)], ones16[:, 0:nk]], axis=0)
        pv = _dot(vt_aug, pt)
        acc_ref[qi, j, hc] = pv if first else jnp.exp2(m_old - m_new) * acc_ref[qi, j, hc] + pv
        m_ref[qi, j, hc] = m_new

    def step(cur, nxt, b, diag):
        for part in parts:
            if nxt is not None:
                logits(nxt, bufs[1 - b], diag, part)
            update(cur, bufs[b], diag, part)

    def sweep(n, start, succ, diag):
        if n == 0:
            return
        for part in parts:
            logits(start, bufs[0], diag, part)

        def trip(_, c):
            for i in range(FOX_UNROLL):
                cn = succ(c)
                step(c, cn, i % 2, diag)
                c = cn
            return c

        trips = (n - 1) // FOX_UNROLL
        c = lax.fori_loop(0, trips, trip, start)
        rest = n - trips * FOX_UNROLL
        for i in range(rest):
            cn = succ(c) if i < rest - 1 else None
            step(c, cn, i % 2, diag)
            c = cn

    zero = jnp.int32(0)
    sweep(n_q, (zero, zero), lambda c: (c[0] + 1, c[1] + 1), True)

    def below(c):
        wrap = c[1] + 1 >= c[0]
        return jnp.where(wrap, c[0] + 1, c[0]), jnp.where(wrap, 0, c[1] + 1)

    sweep(n_q * (n_q - 1) // 2, (zero + 1, zero), below, False)

    def finish(qi, _):
        ys = []
        for j in range(FOX_HEADS):
            acc = jnp.concatenate([acc_ref[qi, j, hc] for hc in range(n_strips)], axis=1)
            o = acc[0:DH_F] * (1.0 / acc[DH_F:DH_F + 1])
            ys.append(o * lax.rsqrt(jnp.mean(o * o, axis=0, keepdims=True) + EPS)
                      * jnp.tile(gain_ref[DH_F * j:DH_F * (j + 1), :], (1, T // LANES)))
        o_ref[pl.ds(pl.multiple_of(qi * T, T), T), :] = jnp.concatenate(ys, axis=0).T.astype(o_ref.dtype)
        return 0

    lax.fori_loop(0, n_q, finish, 0, unroll=FOX_FINISH_UNROLL)


def _fox(fq, fk, fvt, gain):
    B, S, _ = fq.shape
    n_pair = NH_F // FOX_HEADS
    n_strips = T_F // MXU_COLS
    tok = pl.BlockSpec((None, S, FOX_HEADS * LANES), lambda b, p: (b, 0, p))
    return pl.pallas_call(
        _fox_kernel,
        grid=(B, n_pair),
        in_specs=[tok, tok,
                  pl.BlockSpec((None, FOX_HEADS * DH_F, S), lambda b, p: (b, p, 0)),
                  pl.BlockSpec((FOX_HEADS * DH_F, LANES), lambda b, p: (p, 0))],
        out_specs=pl.BlockSpec((None, S, FOX_HEADS * DH_F), lambda b, p: (b, 0, p)),
        out_shape=jax.ShapeDtypeStruct((B, S, D_FV), BF16),
        scratch_shapes=[pltpu.VMEM((FOX_HEADS, n_strips, T_F, MXU_COLS), F32), pltpu.VMEM((FOX_HEADS, n_strips, T_F, MXU_COLS), F32),
                        pltpu.VMEM((FOX_HEADS, n_strips, 1, MXU_COLS), F32), pltpu.VMEM((FOX_HEADS, n_strips, 1, MXU_COLS), F32),
                        pltpu.VMEM((S // T_F, FOX_HEADS, n_strips, 1, MXU_COLS), F32),
                        pltpu.VMEM((S // T_F, FOX_HEADS, n_strips, DH_F + BF16_ROWS, MXU_COLS), F32),
                        pltpu.VMEM((n_strips, T_F, MXU_COLS), F32)],
        compiler_params=pltpu.CompilerParams(
            dimension_semantics=("arbitrary", "arbitrary"), vmem_limit_bytes=VMEM_LIMIT),
        name="fox",
    )(fq, fk, fvt, gain)


def _mix_out_kernel(h_ref, ym_ref, yf_ref, p_ref, wom_ref, wof_ref, wg_ref, wu_ref, wd_ref,
                    wpg_ref, wpp_ref, npp_ref, nf_ref, o_ref):
    def unit(i):
        rows = slice(i * UNIT, (i + 1) * UNIT)
        h = h_ref[rows, :] + _dot(ym_ref[rows, :], wom_ref[...]) + _dot(yf_ref[rows, :], wof_ref[...])
        pe = _rms(_dot(p_ref[rows, :].astype(BF16), wpp_ref[...]), npp_ref[...])
        xn = _unit_rms(h).astype(BF16)
        yield
        acc = None
        for c in range(0, D_FF, FF_CHUNK):
            part = _swiglu_chunk(xn, c, wg_ref, wu_ref, wd_ref)
            acc = part if acc is None else acc + part
            yield
        h = h + acc
        xg = _unit_rms(h).astype(BF16)
        yield
        h = h + jax.nn.sigmoid(_dot(xg, wpg_ref[...])) * pe
        o_ref[rows, :] = _rms(h, nf_ref[...])
        yield

    _skew([unit(i) for i in range(h_ref.shape[0] // UNIT)], lag=N_FF_CHUNKS + 1, early=N_FF_CHUNKS - 2)


def _mix_out(h1, ym, yf, p, *consts):
    B, S, D = h1.shape
    tok = lambda w: pl.BlockSpec((None, TM_OUT, w), lambda b, i: (b, i, 0))
    return pl.pallas_call(
        _mix_out_kernel,
        grid=(B, S // TM_OUT),
        in_specs=[tok(D), tok(D_MV), tok(D_FV), tok(p.shape[-1])] + [_const_spec(c.shape) for c in consts],
        out_specs=tok(D),
        out_shape=jax.ShapeDtypeStruct((B, S, D), F32),
        compiler_params=pltpu.CompilerParams(
            dimension_semantics=("arbitrary", "arbitrary"), vmem_limit_bytes=VMEM_LIMIT),
        name="mix_out",
    )(h1, ym, yf, p, *consts)


_O_MQK = 0
_O_MV = _O_MQK + D_MQK
_O_MO = _O_MV + D_MV
_O_MIF = _O_MO + D_MV
_O_FQ = _O_MIF + 2 * NH_M
_O_FK = _O_FQ + D_FV
_O_FV = _O_FK + D_FV
_O_FF = _O_FV + D_FV
_NAT_COLS = np.concatenate([np.arange(_O_MQK, _O_MQK + D_MQK), np.arange(_O_MO, _O_MO + D_MV),
                            np.arange(_O_FQ, _O_FQ + D_FV), np.arange(_O_FK, _O_FK + D_FV)])
_T_COLS = np.concatenate([np.arange(_O_MV, _O_MV + D_MV), np.arange(_O_FV, _O_FV + D_FV),
                          np.arange(_O_MIF, _O_MIF + 2 * NH_M), np.arange(_O_FF, _O_FF + NH_F)])


def _placement():
    e = np.zeros((3 * N_GATES, NH_F * LANES), np.float32)
    for j in range(3):
        for h in range(NH_F):
            e[N_GATES * j + 2 * NH_M + h, LANES * h + (j if h % 2 else DH_F + j)] = 1.0
    return e


def _tri(n, block):
    s = np.arange(n)[:, None]
    t = np.arange(n)[None, :]
    return ((s <= t) & (s // block == t // block)).astype(np.float32)


def _take_cols(w, cols):
    cuts = [0] + [i for i in range(1, len(cols)) if cols[i] != cols[i - 1] + 1] + [len(cols)]
    return jnp.concatenate([w[:, int(cols[a]):int(cols[b - 1]) + 1] for a, b in zip(cuts[:-1], cuts[1:])], axis=1)


def kernel(x, p, ffn1_norm, ffn1_w_gate, ffn1_w_up, ffn1_w_down, mix_norm, w_in, conv_qk, b_mlstm_gates, b_fox_f, mlstm_out_norm, fox_out_norm, w_out, ffn2_norm, ffn2_w_gate, ffn2_w_up, ffn2_w_down, ple_gate_norm, w_ple_gate, w_ple_proj, ple_proj_norm, final_norm):
    depth = w_in.shape[0]
    assert depth == 1, "the final RMSNorm is fused into the (single) layer's last kernel"
    assert x.shape[1] % TM_OUT == 0 and x.shape[1] % (L_M * MLSTM_CHUNKS) == 0 and x.shape[2] == D_MODEL
    assert TM_IN % UNIT == 0 and TM_OUT % UNIT == 0 and UNIT % L_M == 0 and FOX_UNROLL % 2 == 0
    row = lambda v: v.reshape(1, -1).astype(F32)
    col = lambda v: jnp.broadcast_to(v.astype(F32)[:, None], (v.shape[0], LANES))
    ucum = jnp.asarray(_tri(UNIT, UNIT), BF16)
    ubd = jnp.asarray(_tri(UNIT, L_M), BF16)
    e48 = jnp.asarray(_placement(), BF16)
    h = x
    for i in range(depth):
        def scaled(w, g):
            return (g.astype(F32)[:, None] * w).astype(BF16)

        wnat = scaled(_take_cols(w_in[i], _NAT_COLS), mix_norm[i])
        wt = scaled(_take_cols(w_in[i], _T_COLS), mix_norm[i]).T
        conv = conv_qk[i].astype(F32)
        bias16 = col(jnp.concatenate([b_mlstm_gates[i], b_fox_f[i]]))
        h1, mq, mk, mo, mvt, mg, fq, fk, fvt = _mix_in(
            h, scaled(ffn1_w_gate[i], ffn1_norm[i]), scaled(ffn1_w_up[i], ffn1_norm[i]),
            (0.5 * ffn1_w_down[i]).astype(BF16), wnat, wt, conv, bias16, ucum, ubd, e48)
        ym = _mlstm(mq, mk, mvt, mg, mo, col(mlstm_out_norm[i]))
        yf = _fox(fq, fk, fvt, col(fox_out_norm[i]))
        wo = w_out[i].astype(BF16)
        h = _mix_out(
            h1, ym, yf, p[i], wo[:D_MV], wo[D_MV:], scaled(ffn2_w_gate[i], ffn2_norm[i]),
            scaled(ffn2_w_up[i], ffn2_norm[i]), (0.5 * ffn2_w_down[i]).astype(BF16),
            scaled(w_ple_gate[i], ple_gate_norm[i]), w_ple_proj[i].astype(BF16), row(ple_proj_norm[i]),
            row(final_norm))
    return h
```

```python
import numpy as np
import jax
import jax.numpy as jnp
from jax import lax
from jax.experimental import pallas as pl
from jax.experimental.pallas import tpu as pltpu

F32 = jnp.float32
BF16 = jnp.bfloat16

EPS = 1e-6
D_MODEL = 1024
D_FF = 2816
NH_M = 4
DV_M = 128
DK_M = 64
CONV_W = 4
NH_F = 8
DH_F = 64
D_MV = NH_M * DV_M
D_FV = NH_F * DH_F
D_MQK = 2 * NH_M * DK_M
N_GATES = 2 * NH_M + NH_F

LANES = 128
SUBLANES = 8
BF16_ROWS = 16
MXU_COLS = 256
TM_IN = 512
TM_OUT = 1024
UNIT = 512
W_STAGE_ROWS = 256
FF_CHUNK = 256
N_FF_CHUNKS = D_FF // FF_CHUNK
L_M = 256
MLSTM_CHUNKS = 8
T_F = 512
LOG2E = 1.4426950408889634
FOX_HEADS = 4
FOX_UNROLL = 8
FOX_FINISH_UNROLL = 4
NEG = -1e30
VMEM_LIMIT = 60 * 1024 * 1024

NT = (((1,), (1,)), ((), ()))
TN = (((0,), (0,)), ((), ()))


def _unit_rms(x):
    return x * lax.rsqrt(jnp.mean(x * x, axis=-1, keepdims=True) + EPS)


def _rms(x, g):
    return _unit_rms(x) * g


def _dot(a, b):
    return jnp.dot(a, b, preferred_element_type=F32)


def _split3(x):
    hi = x.astype(BF16)
    r = x - hi.astype(F32)
    mid = r.astype(BF16)
    lo = (r - mid.astype(F32)).astype(BF16)
    return hi, mid, lo


def _swiglu_chunk(xn, c, wg_ref, wu_ref, wd_ref):
    a = _dot(xn, wg_ref[:, c:c + FF_CHUNK])
    b = _dot(xn, wu_ref[:, c:c + FF_CHUNK])
    return _dot((a * jax.nn.sigmoid(a) * b).astype(BF16), wd_ref[c:c + FF_CHUNK, :])


def _skew(units, lag, early):
    done = [0] * len(units)
    tick = 0
    while any(d >= 0 for d in done):
        for i, u in enumerate(units):
            due = max(0, i * lag - early) if done[i] == 0 else i * lag + done[i]
            if done[i] >= 0 and tick >= due:
                try:
                    next(u)
                    done[i] += 1
                except StopIteration:
                    done[i] = -1
        tick += 1


def _stage_ffn_weights(jobs, stage, sem):
    def copy(j):
        src, r, _ = jobs[j]
        buf = stage[src.shape[1]]
        return pltpu.make_async_copy(src.at[pl.ds(r, W_STAGE_ROWS), :], buf.at[j % 2], sem.at[j % 2])

    copy(0).start()
    for j, (src, r, convert) in enumerate(jobs):
        if j + 1 < len(jobs):
            copy(j + 1).start()
        copy(j).wait()
        convert(stage[src.shape[1]][j % 2])


def _mix_in_kernel(x_ref, wg_hbm, wu_hbm, wd_hbm, g1_ref, wnat_ref, wt_ref, conv_ref,
                   bias_ref, ucum_ref, ubd_ref, e48_ref,
                   h1_ref, mq_ref, mk_ref, mo_ref, mvt_ref, mg_ref, fq_ref, fk_ref, fvt_ref,
                   cbuf, carry, wg_ref, wu_ref, wd_ref, stage_ff, stage_dm, sem):
    tm = x_ref.shape[0]
    n_units = tm // UNIT

    @pl.when((pl.program_id(0) == 0) & (pl.program_id(1) == 0))
    def _():
        def up(dst, r):
            def convert(blk):
                g = jnp.tile(g1_ref[r:r + W_STAGE_ROWS, :], (1, D_FF // LANES))
                dst[r:r + W_STAGE_ROWS, :] = (g * blk).astype(BF16)
            return convert

        def down(r):
            def convert(blk):
                wd_ref[r:r + W_STAGE_ROWS, :] = (0.5 * blk).astype(BF16)
            return convert

        jobs = ([(wg_hbm, r, up(wg_ref, r)) for r in range(0, D_MODEL, W_STAGE_ROWS)]
                + [(wu_hbm, r, up(wu_ref, r)) for r in range(0, D_MODEL, W_STAGE_ROWS)]
                + [(wd_hbm, r, down(r)) for r in range(0, D_FF, W_STAGE_ROWS)])
        _stage_ffn_weights(jobs, {D_FF: stage_ff, D_MODEL: stage_dm}, sem)

    @pl.when(pl.program_id(1) == 0)
    def _():
        cbuf[0:SUBLANES, :] = jnp.zeros((SUBLANES, cbuf.shape[1]), F32)
        carry[...] = jnp.zeros(carry.shape, F32)

    lane = lax.broadcasted_iota(jnp.int32, (UNIT, LANES), 1)
    low = lane < DK_M
    ones_lo = jnp.where((lane >= DH_F) & (lane < DH_F + 3), 1.0, 0.0)
    ones_hi = jnp.where(lane < 3, 1.0, 0.0)
    r8 = lax.broadcasted_iota(jnp.int32, (2 * NH_M, UNIT), 0)

    def groups(z, fill):
        out = []
        for j in range(z.shape[1] // LANES):
            c = z[:, LANES * j:LANES * (j + 1)]
            out += [jnp.where(low, c, fill(2 * j)), jnp.where(low, fill(2 * j + 1), c)]
        return out

    def store_groups(ref, rows, gs):
        for g, v in enumerate(gs):
            ref[rows, LANES * g:LANES * (g + 1)] = v.astype(ref.dtype)

    def fold(r):
        return r[0:N_GATES] + r[N_GATES:2 * N_GATES] + r[2 * N_GATES:3 * N_GATES]

    zero = lambda j: 0.0

    def unit(i):
        r0 = i * UNIT
        rows = slice(r0, r0 + UNIT)
        x = x_ref[rows, :]
        xn = _unit_rms(x).astype(BF16)
        yield
        acc = None
        for c in range(0, D_FF, FF_CHUNK):
            part = _swiglu_chunk(xn, c, wg_ref, wu_ref, wd_ref)
            acc = part if acc is None else acc + part
            yield
        h1 = x + acc
        h1_ref[rows, :] = h1
        u = _unit_rms(h1).astype(BF16)
        yield

        zqk = _dot(u, wnat_ref[:, 0:D_MQK])
        cbuf[SUBLANES + r0:SUBLANES + r0 + UNIT, :] = zqk
        yield
        cv = conv_ref[CONV_W - 1:CONV_W, :] * zqk
        for j in range(CONV_W - 1):
            top = SUBLANES + r0 - (CONV_W - 1) + j
            cv = cv + conv_ref[j:j + 1, :] * cbuf[top:top + UNIT, :]
        if i == n_units - 1:
            cbuf[0:SUBLANES, :] = cbuf[tm:tm + SUBLANES, :]
        s = cv * jax.nn.sigmoid(cv)
        store_groups(mq_ref, rows, groups(s[:, 0:D_MQK // 2] * (DK_M ** -0.5), zero))
        store_groups(mk_ref, rows, groups(s[:, D_MQK // 2:D_MQK], zero))
        yield

        mo_ref[rows, :] = _dot(u, wnat_ref[:, D_MQK:D_MQK + D_MV]).astype(BF16)
        yield

        zt = lax.dot_general(wt_ref[...], u, NT, preferred_element_type=F32)
        mvt_ref[:, rows] = zt[0:D_MV].astype(BF16)
        fvt_ref[:, rows] = zt[D_MV:D_MV + D_FV].astype(BF16)
        g16 = zt[D_MV + D_FV:D_MV + D_FV + N_GATES] + jnp.tile(bias_ref[...], (1, UNIT // LANES))
        yield
        lf = jnp.minimum(g16, 0.0) - jnp.log1p(jnp.exp(-jnp.abs(g16)))
        xs = jnp.concatenate(_split3(lf), axis=0)
        cum = fold(_dot(xs, ucum_ref[...]))
        bch = fold(_dot(xs, ubd_ref[...]))
        tot = fold(_dot(xs, jnp.ones((UNIT, LANES), BF16)))
        c16 = cum + jnp.tile(carry[...], (1, UNIT // LANES))
        carry[...] = carry[...] + tot
        mg_ref[:, rows] = jnp.where(r8 < NH_M, g16[0:2 * NH_M], bch[0:2 * NH_M])
        ys = jnp.concatenate(_split3(-LOG2E * c16), axis=0)
        place = lax.dot_general(ys, e48_ref[...], TN, preferred_element_type=F32)
        yield
        zfq = _dot(u, wnat_ref[:, D_MQK + D_MV:D_MQK + D_MV + D_FV]) * (DH_F ** -0.5 * LOG2E)
        store_groups(fq_ref, rows, groups(zfq, lambda g: ones_hi if g % 2 else ones_lo))
        yield
        zfk = _dot(u, wnat_ref[:, D_MQK + D_MV + D_FV:D_MQK + D_MV + 2 * D_FV])
        store_groups(fk_ref, rows, groups(zfk, lambda g: place[:, LANES * g:LANES * (g + 1)]))
        yield

    _skew([unit(i) for i in range(n_units)], lag=N_FF_CHUNKS + 1, early=N_FF_CHUNKS - 2)


def _const_spec(shape):
    nd = len(shape)
    return pl.BlockSpec(shape, lambda *_: (0,) * nd, pipeline_mode=pl.Buffered(1))


def _mix_in(x, wg, wu, wd, g1, wnat, wt, conv, bias16, ucum, ubd, e48):
    B, S, D = x.shape
    n_t = S // TM_IN
    tok = lambda w: pl.BlockSpec((None, TM_IN, w), lambda b, i: (b, i, 0))
    tokt = lambda r: pl.BlockSpec((None, r, TM_IN), lambda b, i: (b, 0, i))
    consts = (g1, wnat, wt, conv, bias16, ucum, ubd, e48)
    hbm = pl.BlockSpec(memory_space=pl.ANY)
    out_shape = (
        jax.ShapeDtypeStruct((B, S, D), F32),
        jax.ShapeDtypeStruct((B, S, NH_M * LANES), BF16),
        jax.ShapeDtypeStruct((B, S, NH_M * LANES), BF16),
        jax.ShapeDtypeStruct((B, S, D_MV), BF16),
        jax.ShapeDtypeStruct((B, D_MV, S), BF16),
        jax.ShapeDtypeStruct((B, 2 * NH_M, S), F32),
        jax.ShapeDtypeStruct((B, S, NH_F * LANES), BF16),
        jax.ShapeDtypeStruct((B, S, NH_F * LANES), BF16),
        jax.ShapeDtypeStruct((B, D_FV, S), BF16),
    )
    out_specs = (tok(D), tok(NH_M * LANES), tok(NH_M * LANES), tok(D_MV), tokt(D_MV), tokt(2 * NH_M),
                 tok(NH_F * LANES), tok(NH_F * LANES), tokt(D_FV))
    return pl.pallas_call(
        _mix_in_kernel,
        grid=(B, n_t),
        in_specs=[tok(D), hbm, hbm, hbm] + [_const_spec(c.shape) for c in consts],
        out_specs=out_specs,
        out_shape=out_shape,
        scratch_shapes=[pltpu.VMEM((TM_IN + SUBLANES, D_MQK), F32), pltpu.VMEM((N_GATES, LANES), F32),
                        pltpu.VMEM((D, D_FF), BF16), pltpu.VMEM((D, D_FF), BF16), pltpu.VMEM((D_FF, D), BF16),
                        pltpu.VMEM((2, W_STAGE_ROWS, D_FF), F32), pltpu.VMEM((2, W_STAGE_ROWS, D), F32),
                        pltpu.SemaphoreType.DMA((2,))],
        compiler_params=pltpu.CompilerParams(
            dimension_semantics=("arbitrary", "arbitrary"), vmem_limit_bytes=VMEM_LIMIT),
        name="mix_in",
    )(x, wg, wu, wd, *consts)


def _mlstm_kernel(q_ref, k_ref, vt_ref, mg_ref, mo_ref, gain_ref, o_ref, ct_ref):
    S = q_ref.shape[0]
    L = L_M
    ct_ref[...] = jnp.zeros(ct_ref.shape, F32)
    causal = (lax.broadcasted_iota(jnp.int32, (L, L), 0) <= lax.broadcasted_iota(jnp.int32, (L, L), 1))
    r16 = lax.broadcasted_iota(jnp.int32, (BF16_ROWS, L), 0)
    ones_rows = jnp.where(r16 < 3, 1.0, 0.0).astype(BF16)
    ones16 = jnp.ones((BF16_ROWS, L), BF16)

    def head_chunk(cs, h, ms):
        hs = slice(LANES * h, LANES * (h + 1))
        m = ms[h]
        logi = LOG2E * mg_ref[h:h + 1, pl.ds(cs, L)]
        b = LOG2E * mg_ref[NH_M + h:NH_M + h + 1, pl.ds(cs, L)]
        a = logi - b
        gtot = b[:, L - 1:L]
        m_new = jnp.maximum(gtot + m, gtot + jnp.max(a, axis=1, keepdims=True))
        ms[h] = m_new
        ahi, amid, alo = (p.astype(F32) for p in _split3(a))
        at = jnp.where(r16 == 0, ahi, jnp.where(r16 == 1, amid, jnp.where(r16 == 2, alo, 0.0)))
        acol = lax.dot_general(at.astype(BF16), ones_rows, TN, preferred_element_type=F32)
        q = q_ref[pl.ds(cs, L), hs]
        k = k_ref[pl.ds(cs, L), hs]
        st = lax.dot_general(k, q, NT, preferred_element_type=F32)
        ct = ct_ref[h]
        inter = lax.dot_general(ct.astype(BF16), q, NT, preferred_element_type=F32)
        yield
        e = jnp.where(causal, acol, -jnp.inf)
        mu = jnp.maximum(m, jnp.max(e, axis=0, keepdims=True))
        pt = (jnp.exp2(e - mu) * st).astype(BF16)
        vt_aug = jnp.concatenate([vt_ref[hs, pl.ds(cs, L)], ones16], axis=0)
        w = jnp.exp2(a + gtot - m_new)
        vtw = vt_aug * w.astype(BF16)
        yield
        nd = jnp.exp2(m - mu) * inter + _dot(vt_aug, pt)
        ct_ref[h] = jnp.exp2(gtot + m - m_new) * ct + _dot(vtw, k)
        yield
        rden = 1.0 / jnp.maximum(jnp.abs(nd[DV_M:DV_M + 1]), jnp.exp2(-(b + mu)))
        ht = nd[0:DV_M] * rden
        yt = (ht * lax.rsqrt(jnp.mean(ht * ht, axis=0, keepdims=True) + EPS)
              * jnp.tile(gain_ref[hs, :], (1, L // LANES)))
        og = jax.nn.sigmoid(mo_ref[pl.ds(cs, L), hs].astype(F32))
        o_ref[pl.ds(cs, L), hs] = (yt.T * og).astype(o_ref.dtype)
        yield

    def trip(c, ms):
        ms = list(ms)
        starts = [pl.multiple_of((c * MLSTM_CHUNKS + j) * L, L) for j in range(MLSTM_CHUNKS)]
        _skew([head_chunk(cs, h, ms) for cs in starts for h in range(NH_M)], lag=1, early=0)
        return tuple(ms)

    lax.fori_loop(0, S // (L * MLSTM_CHUNKS), trip, tuple(jnp.zeros((1, 1), F32) for _ in range(NH_M)))


def _mlstm(mq, mk, mvt, mg, mo, gain):
    B, S, _ = mq.shape
    tok = pl.BlockSpec((None, S, NH_M * LANES), lambda b: (b, 0, 0))
    return pl.pallas_call(
        _mlstm_kernel,
        grid=(B,),
        in_specs=[tok, tok,
                  pl.BlockSpec((None, D_MV, S), lambda b: (b, 0, 0)),
                  pl.BlockSpec((None, 2 * NH_M, S), lambda b: (b, 0, 0)),
                  pl.BlockSpec((None, S, D_MV), lambda b: (b, 0, 0)),
                  _const_spec(gain.shape)],
        out_specs=pl.BlockSpec((None, S, D_MV), lambda b: (b, 0, 0)),
        out_shape=jax.ShapeDtypeStruct((B, S, D_MV), BF16),
        scratch_shapes=[pltpu.VMEM((NH_M, DV_M + BF16_ROWS, LANES), F32)],
        compiler_params=pltpu.CompilerParams(
            dimension_semantics=("arbitrary",), vmem_limit_bytes=VMEM_LIMIT),
        name="mlstm",
    )(mq, mk, mvt, mg, mo, gain)


def _fox_kernel(q_ref, k_ref, vt_ref, gain_ref, o_ref, st_a, st_b, mb_a, mb_b, m_ref, acc_ref, bias_ref):
    S = q_ref.shape[0]
    T = T_F
    n_q = S // T
    heads = [slice(LANES * j, LANES * (j + 1)) for j in range(FOX_HEADS)]
    bufs = ((st_a, mb_a), (st_b, mb_b))
    n_strips = T // MXU_COLS
    parts = [(j, hc) for j in range(FOX_HEADS) for hc in range(n_strips)]
    ones16 = jnp.ones((BF16_ROWS, T), BF16)
    for hc in range(n_strips):
        visible = (lax.broadcasted_iota(jnp.int32, (T, MXU_COLS), 0)
                   <= lax.broadcasted_iota(jnp.int32, (T, MXU_COLS), 1) + hc * MXU_COLS)
        bias_ref[hc] = jnp.where(visible, 0.0, NEG)

    def n_keys(diag, hc):
        return (hc + 1) * MXU_COLS if diag else T

    def logits(c, buf, diag, part):
        qi, kj = c
        st_buf, mb_buf = buf
        j, hc = part
        nk = n_keys(diag, hc)
        qs = pl.multiple_of(qi * T + hc * MXU_COLS, MXU_COLS)
        ks = pl.multiple_of(kj * T, T)
        st = lax.dot_general(k_ref[pl.ds(ks, nk), heads[j]], q_ref[pl.ds(qs, MXU_COLS), heads[j]], NT,
                             preferred_element_type=F32)
        if diag:
            st = st + bias_ref[hc, 0:nk, :]
        st_buf[j, hc, 0:nk, :] = st
        mb_buf[j, hc] = jnp.max(st, axis=0, keepdims=True)

    def update(c, buf, first, part):
        qi, kj = c
        st_buf, mb_buf = buf
        j, hc = part
        nk = n_keys(first, hc)
        ks = pl.multiple_of(kj * T, T)
        m_old = m_ref[qi, j, hc]
        m_new = mb_buf[j, hc] if first else jnp.maximum(m_old, mb_buf[j, hc])
        pt = jnp.exp2(st_buf[j, hc, 0:nk, :] - m_new).astype(BF16)
        vt_aug = jnp.concatenate([vt_ref[DH_F * j:DH_F * (j + 1), pl.ds(ks, nk)], ones16[:, 0:nk]], axis=0)
        pv = _dot(vt_aug, pt)
        acc_ref[qi, j, hc] = pv if first else jnp.exp2(m_old - m_new) * acc_ref[qi, j, hc] + pv
        m_ref[qi, j, hc] = m_new

    def step(cur, nxt, b, diag):
        for part in parts:
            if nxt is not None:
                logits(nxt, bufs[1 - b], diag, part)
            update(cur, bufs[b], diag, part)

    def sweep(n, start, succ, diag):
        if n == 0:
            return
        for part in parts:
            logits(start, bufs[0], diag, part)

        def trip(_, c):
            for i in range(FOX_UNROLL):
                cn = succ(c)
                step(c, cn, i % 2, diag)
                c = cn
            return c

        trips = (n - 1) // FOX_UNROLL
        c = lax.fori_loop(0, trips, trip, start)
        rest = n - trips * FOX_UNROLL
        for i in range(rest):
            cn = succ(c) if i < rest - 1 else None
            step(c, cn, i % 2, diag)
            c = cn

    zero = jnp.int32(0)
    sweep(n_q, (zero, zero), lambda c: (c[0] + 1, c[1] + 1), True)

    def below(c):
        wrap = c[1] + 1 >= c[0]
        return jnp.where(wrap, c[0] + 1, c[0]), jnp.where(wrap, 0, c[1] + 1)

    sweep(n_q * (n_q - 1) // 2, (zero + 1, zero), below, False)

    def finish(qi, _):
        ys = []
        for j in range(FOX_HEADS):
            acc = jnp.concatenate([acc_ref[qi, j, hc] for hc in range(n_strips)], axis=1)
            o = acc[0:DH_F] * (1.0 / acc[DH_F:DH_F + 1])
            ys.append(o * lax.rsqrt(jnp.mean(o * o, axis=0, keepdims=True) + EPS)
                      * jnp.tile(gain_ref[DH_F * j:DH_F * (j + 1), :], (1, T // LANES)))
        o_ref[pl.ds(pl.multiple_of(qi * T, T), T), :] = jnp.concatenate(ys, axis=0).T.astype(o_ref.dtype)
        return 0

    lax.fori_loop(0, n_q, finish, 0, unroll=FOX_FINISH_UNROLL)


def _fox(fq, fk, fvt, gain):
    B, S, _ = fq.shape
    n_pair = NH_F // FOX_HEADS
    n_strips = T_F // MXU_COLS
    tok = pl.BlockSpec((None, S, FOX_HEADS * LANES), lambda b, p: (b, 0, p))
    return pl.pallas_call(
        _fox_kernel,
        grid=(B, n_pair),
        in_specs=[tok, tok,
                  pl.BlockSpec((None, FOX_HEADS * DH_F, S), lambda b, p: (b, p, 0)),
                  pl.BlockSpec((FOX_HEADS * DH_F, LANES), lambda b, p: (p, 0))],
        out_specs=pl.BlockSpec((None, S, FOX_HEADS * DH_F), lambda b, p: (b, 0, p)),
        out_shape=jax.ShapeDtypeStruct((B, S, D_FV), BF16),
        scratch_shapes=[pltpu.VMEM((FOX_HEADS, n_strips, T_F, MXU_COLS), F32), pltpu.VMEM((FOX_HEADS, n_strips, T_F, MXU_COLS), F32),
                        pltpu.VMEM((FOX_HEADS, n_strips, 1, MXU_COLS), F32), pltpu.VMEM((FOX_HEADS, n_strips, 1, MXU_COLS), F32),
                        pltpu.VMEM((S // T_F, FOX_HEADS, n_strips, 1, MXU_COLS), F32),
                        pltpu.VMEM((S // T_F, FOX_HEADS, n_strips, DH_F + BF16_ROWS, MXU_COLS), F32),
                        pltpu.VMEM((n_strips, T_F, MXU_COLS), F32)],
        compiler_params=pltpu.CompilerParams(
            dimension_semantics=("arbitrary", "arbitrary"), vmem_limit_bytes=VMEM_LIMIT),
        name="fox",
    )(fq, fk, fvt, gain)


def _mix_out_kernel(h_ref, ym_ref, yf_ref, p_ref, wom_ref, wof_ref, wg_ref, wu_ref, wd_ref,
                    wpg_ref, wpp_ref, npp_ref, nf_ref, o_ref):
    def unit(i):
        rows = slice(i * UNIT, (i + 1) * UNIT)
        h = h_ref[rows, :] + _dot(ym_ref[rows, :], wom_ref[...]) + _dot(yf_ref[rows, :], wof_ref[...])
        pe = _rms(_dot(p_ref[rows, :].astype(BF16), wpp_ref[...]), npp_ref[...])
        xn = _unit_rms(h).astype(BF16)
        yield
        acc = None
        for c in range(0, D_FF, FF_CHUNK):
            part = _swiglu_chunk(xn, c, wg_ref, wu_ref, wd_ref)
            acc = part if acc is None else acc + part
            yield
        h = h + acc
        xg = _unit_rms(h).astype(BF16)
        yield
        h = h + jax.nn.sigmoid(_dot(xg, wpg_ref[...])) * pe
        o_ref[rows, :] = _rms(h, nf_ref[...])
        yield

    _skew([unit(i) for i in range(h_ref.shape[0] // UNIT)], lag=N_FF_CHUNKS + 1, early=N_FF_CHUNKS - 2)


def _mix_out(h1, ym, yf, p, *consts):
    B, S, D = h1.shape
    tok = lambda w: pl.BlockSpec((None, TM_OUT, w), lambda b, i: (b, i, 0))
    return pl.pallas_call(
        _mix_out_kernel,
        grid=(B, S // TM_OUT),
        in_specs=[tok(D), tok(D_MV), tok(D_FV), tok(p.shape[-1])] + [_const_spec(c.shape) for c in consts],
        out_specs=tok(D),
        out_shape=jax.ShapeDtypeStruct((B, S, D), F32),
        compiler_params=pltpu.CompilerParams(
            dimension_semantics=("arbitrary", "arbitrary"), vmem_limit_bytes=VMEM_LIMIT),
        name="mix_out",
    )(h1, ym, yf, p, *consts)


_O_MQK = 0
_O_MV = _O_MQK + D_MQK
_O_MO = _O_MV + D_MV
_O_MIF = _O_MO + D_MV
_O_FQ = _O_MIF + 2 * NH_M
_O_FK = _O_FQ + D_FV
_O_FV = _O_FK + D_FV
_O_FF = _O_FV + D_FV
_NAT_COLS = np.concatenate([np.arange(_O_MQK, _O_MQK + D_MQK), np.arange(_O_MO, _O_MO + D_MV),
                            np.arange(_O_FQ, _O_FQ + D_FV), np.arange(_O_FK, _O_FK + D_FV)])
_T_COLS = np.concatenate([np.arange(_O_MV, _O_MV + D_MV), np.arange(_O_FV, _O_FV + D_FV),
                          np.arange(_O_MIF, _O_MIF + 2 * NH_M), np.arange(_O_FF, _O_FF + NH_F)])


def _placement():
    e = np.zeros((3 * N_GATES, NH_F * LANES), np.float32)
    for j in range(3):
        for h in range(NH_F):
            e[N_GATES * j + 2 * NH_M + h, LANES * h + (j if h % 2 else DH_F + j)] = 1.0
    return e


def _tri(n, block):
    s = np.arange(n)[:, None]
    t = np.arange(n)[None, :]
    return ((s <= t) & (s // block == t // block)).astype(np.float32)


def _take_cols(w, cols):
    cuts = [0] + [i for i in range(1, len(cols)) if cols[i] != cols[i - 1] + 1] + [len(cols)]
    return jnp.concatenate([w[:, int(cols[a]):int(cols[b - 1]) + 1] for a, b in zip(cuts[:-1], cuts[1:])], axis=1)


def kernel(x, p, ffn1_norm, ffn1_w_gate, ffn1_w_up, ffn1_w_down, mix_norm, w_in, conv_qk, b_mlstm_gates, b_fox_f, mlstm_out_norm, fox_out_norm, w_out, ffn2_norm, ffn2_w_gate, ffn2_w_up, ffn2_w_down, ple_gate_norm, w_ple_gate, w_ple_proj, ple_proj_norm, final_norm):
    depth = w_in.shape[0]
    assert depth == 1, "the final RMSNorm is fused into the (single) layer's last kernel"
    assert x.shape[1] % TM_OUT == 0 and x.shape[1] % (L_M * MLSTM_CHUNKS) == 0 and x.shape[2] == D_MODEL
    assert TM_IN % UNIT == 0 and TM_OUT % UNIT == 0 and UNIT % L_M == 0 and FOX_UNROLL % 2 == 0
    row = lambda v: v.reshape(1, -1).astype(F32)
    col = lambda v: jnp.broadcast_to(v.astype(F32)[:, None], (v.shape[0], LANES))
    ucum = jnp.asarray(_tri(UNIT, UNIT), BF16)
    ubd = jnp.asarray(_tri(UNIT, L_M), BF16)
    e48 = jnp.asarray(_placement(), BF16)
    h = x
    for i in range(depth):
        def scaled(w, g):
            return (g.astype(F32)[:, None] * w).astype(BF16)

        wnat = scaled(_take_cols(w_in[i], _NAT_COLS), mix_norm[i])
        wt = scaled(_take_cols(w_in[i], _T_COLS), mix_norm[i]).T
        conv = conv_qk[i].astype(F32)
        bias16 = col(jnp.concatenate([b_mlstm_gates[i], b_fox_f[i]]))
        h1, mq, mk, mo, mvt, mg, fq, fk, fvt = _mix_in(
            h, ffn1_w_gate[i], ffn1_w_up[i], ffn1_w_down[i], col(ffn1_norm[i]), wnat, wt, conv, bias16, ucum, ubd, e48)
        ym = _mlstm(mq, mk, mvt, mg, mo, col(mlstm_out_norm[i]))
        yf = _fox(fq, fk, fvt, col(fox_out_norm[i]))
        wo = w_out[i].astype(BF16)
        h = _mix_out(
            h1, ym, yf, p[i], wo[:D_MV], wo[D_MV:], scaled(ffn2_w_gate[i], ffn2_norm[i]),
            scaled(ffn2_w_up[i], ffn2_norm[i]), (0.5 * ffn2_w_down[i]).astype(BF16),
            scaled(w_ple_gate[i], ple_gate_norm[i]), w_ple_proj[i].astype(BF16), row(ple_proj_norm[i]),
            row(final_norm))
    return h
```

```python
import numpy as np
import jax
import jax.numpy as jnp
from jax import lax
from jax.experimental import pallas as pl
from jax.experimental.pallas import tpu as pltpu

F32 = jnp.float32
BF16 = jnp.bfloat16

EPS = 1e-6
D_MODEL = 1024
D_FF = 2816
NH_M = 4
DV_M = 128
DK_M = 64
CONV_W = 4
NH_F = 8
DH_F = 64
D_MV = NH_M * DV_M
D_FV = NH_F * DH_F
D_MQK = 2 * NH_M * DK_M
N_GATES = 2 * NH_M + NH_F

LANES = 128
SUBLANES = 8
BF16_ROWS = 16
MXU_COLS = 256
TM_IN = 512
TM_OUT = 1024
UNIT = 512
W_STAGE_ROWS = 256
FF_CHUNK = 256
N_FF_CHUNKS = D_FF // FF_CHUNK
L_M = 256
MLSTM_CHUNKS = 8
T_F = 512
LOG2E = 1.4426950408889634
FOX_HEADS = 4
FOX_UNROLL = 8
FOX_FINISH_UNROLL = 4
NEG = -1e30
VMEM_LIMIT = 60 * 1024 * 1024

NT = (((1,), (1,)), ((), ()))
TN = (((0,), (0,)), ((), ()))


def _unit_rms(x):
    return x * lax.rsqrt(jnp.mean(x * x, axis=-1, keepdims=True) + EPS)


def _rms(x, g):
    return _unit_rms(x) * g


def _dot(a, b):
    return jnp.dot(a, b, preferred_element_type=F32)


def _split3(x):
    hi = x.astype(BF16)
    r = x - hi.astype(F32)
    mid = r.astype(BF16)
    lo = (r - mid.astype(F32)).astype(BF16)
    return hi, mid, lo


def _swiglu_chunk(xn, c, wg_ref, wu_ref, wd_ref):
    a = _dot(xn, wg_ref[:, c:c + FF_CHUNK])
    b = _dot(xn, wu_ref[:, c:c + FF_CHUNK])
    return _dot((a * jax.nn.sigmoid(a) * b).astype(BF16), wd_ref[c:c + FF_CHUNK, :])


def _skew(units, lag, early):
    done = [0] * len(units)
    tick = 0
    while any(d >= 0 for d in done):
        for i, u in enumerate(units):
            due = max(0, i * lag - early) if done[i] == 0 else i * lag + done[i]
            if done[i] >= 0 and tick >= due:
                try:
                    next(u)
                    done[i] += 1
                except StopIteration:
                    done[i] = -1
        tick += 1


def _stage_ffn_weights(jobs, stage, sem):
    def copy(j):
        src, r, _ = jobs[j]
        buf = stage[src.shape[1]]
        return pltpu.make_async_copy(src.at[pl.ds(r, W_STAGE_ROWS), :], buf.at[j % 2], sem.at[j % 2])

    copy(0).start()
    for j, (src, r, convert) in enumerate(jobs):
        if j + 1 < len(jobs):
            copy(j + 1).start()
        copy(j).wait()
        convert(stage[src.shape[1]][j % 2])


def _mix_in_kernel(x_ref, wg_hbm, wu_hbm, wd_hbm, g1_ref, wnat_ref, wt_ref, conv_ref,
                   bias_ref, ucum_ref, ubd_ref, e48_ref,
                   h1_ref, mq_ref, mk_ref, mo_ref, mvt_ref, mg_ref, fq_ref, fk_ref, fvt_ref,
                   cbuf, carry, wg_ref, wu_ref, wd_ref, stage_ff, stage_dm, sem):
    tm = x_ref.shape[0]
    n_units = tm // UNIT

    @pl.when((pl.program_id(0) == 0) & (pl.program_id(1) == 0))
    def _():
        def up(dst, r):
            def convert(blk):
                g = jnp.tile(g1_ref[r:r + W_STAGE_ROWS, :], (1, D_FF // LANES))
                dst[r:r + W_STAGE_ROWS, :] = (g * blk).astype(BF16)
            return convert

        def down(r):
            def convert(blk):
                wd_ref[r:r + W_STAGE_ROWS, :] = (0.5 * blk).astype(BF16)
            return convert

        jobs = ([(wg_hbm, r, up(wg_ref, r)) for r in range(0, D_MODEL, W_STAGE_ROWS)]
                + [(wu_hbm, r, up(wu_ref, r)) for r in range(0, D_MODEL, W_STAGE_ROWS)]
                + [(wd_hbm, r, down(r)) for r in range(0, D_FF, W_STAGE_ROWS)])
        _stage_ffn_weights(jobs, {D_FF: stage_ff, D_MODEL: stage_dm}, sem)

    @pl.when(pl.program_id(1) == 0)
    def _():
        cbuf[0:SUBLANES, :] = jnp.zeros((SUBLANES, cbuf.shape[1]), F32)
        carry[...] = jnp.zeros(carry.shape, F32)

    lane = lax.broadcasted_iota(jnp.int32, (UNIT, LANES), 1)
    low = lane < DK_M
    ones_lo = jnp.where((lane >= DH_F) & (lane < DH_F + 3), 1.0, 0.0)
    ones_hi = jnp.where(lane < 3, 1.0, 0.0)
    r8 = lax.broadcasted_iota(jnp.int32, (2 * NH_M, UNIT), 0)

    def groups(z, fill):
        out = []
        for j in range(z.shape[1] // LANES):
            c = z[:, LANES * j:LANES * (j + 1)]
            out += [jnp.where(low, c, fill(2 * j)), jnp.where(low, fill(2 * j + 1), c)]
        return out

    def store_groups(ref, rows, gs):
        for g, v in enumerate(gs):
            ref[rows, LANES * g:LANES * (g + 1)] = v.astype(ref.dtype)

    def fold(r):
        return r[0:N_GATES] + r[N_GATES:2 * N_GATES] + r[2 * N_GATES:3 * N_GATES]

    zero = lambda j: 0.0

    def unit(i):
        r0 = i * UNIT
        rows = slice(r0, r0 + UNIT)
        x = x_ref[rows, :]
        xn = _unit_rms(x).astype(BF16)
        yield
        acc = None
        for c in range(0, D_FF, FF_CHUNK):
            part = _swiglu_chunk(xn, c, wg_ref, wu_ref, wd_ref)
            acc = part if acc is None else acc + part
            yield
        h1 = x + acc
        h1_ref[rows, :] = h1
        u = _unit_rms(h1).astype(BF16)
        yield

        zqk = _dot(u, wnat_ref[:, 0:D_MQK])
        cbuf[SUBLANES + r0:SUBLANES + r0 + UNIT, :] = zqk
        yield
        cv = conv_ref[CONV_W - 1:CONV_W, :] * zqk
        for j in range(CONV_W - 1):
            top = SUBLANES + r0 - (CONV_W - 1) + j
            cv = cv + conv_ref[j:j + 1, :] * cbuf[top:top + UNIT, :]
        if i == n_units - 1:
            cbuf[0:SUBLANES, :] = cbuf[tm:tm + SUBLANES, :]
        s = cv * jax.nn.sigmoid(cv)
        store_groups(mq_ref, rows, groups(s[:, 0:D_MQK // 2] * (DK_M ** -0.5), zero))
        store_groups(mk_ref, rows, groups(s[:, D_MQK // 2:D_MQK], zero))
        yield

        mo_ref[rows, :] = _dot(u, wnat_ref[:, D_MQK:D_MQK + D_MV]).astype(BF16)
        yield

        zt = lax.dot_general(wt_ref[...], u, NT, preferred_element_type=F32)
        mvt_ref[:, rows] = zt[0:D_MV].astype(BF16)
        fvt_ref[:, rows] = zt[D_MV:D_MV + D_FV].astype(BF16)
        g16 = zt[D_MV + D_FV:D_MV + D_FV + N_GATES] + jnp.tile(bias_ref[...], (1, UNIT // LANES))
        yield
        lf = jnp.minimum(g16, 0.0) - jnp.log1p(jnp.exp(-jnp.abs(g16)))
        xs = jnp.concatenate(_split3(lf), axis=0)
        cum = fold(_dot(xs, ucum_ref[...]))
        bch = fold(_dot(xs, ubd_ref[...]))
        tot = fold(_dot(xs, jnp.ones((UNIT, LANES), BF16)))
        c16 = cum + jnp.tile(carry[...], (1, UNIT // LANES))
        carry[...] = carry[...] + tot
        mg_ref[:, rows] = jnp.where(r8 < NH_M, g16[0:2 * NH_M], bch[0:2 * NH_M])
        ys = jnp.concatenate(_split3(-LOG2E * c16), axis=0)
        place = lax.dot_general(ys, e48_ref[...], TN, preferred_element_type=F32)
        yield
        zfq = _dot(u, wnat_ref[:, D_MQK + D_MV:D_MQK + D_MV + D_FV]) * (DH_F ** -0.5 * LOG2E)
        store_groups(fq_ref, rows, groups(zfq, lambda g: ones_hi if g % 2 else ones_lo))
        yield
        zfk = _dot(u, wnat_ref[:, D_MQK + D_MV + D_FV:D_MQK + D_MV + 2 * D_FV])
        store_groups(fk_ref, rows, groups(zfk, lambda g: place[:, LANES * g:LANES * (g + 1)]))
        yield

    _skew([unit(i) for i in range(n_units)], lag=N_FF_CHUNKS + 1, early=N_FF_CHUNKS - 2)


def _const_spec(shape):
    nd = len(shape)
    return pl.BlockSpec(shape, lambda *_: (0,) * nd, pipeline_mode=pl.Buffered(1))


def _mix_in(x, wg, wu, wd, g1, wnat, wt, conv, bias16, ucum, ubd, e48):
    B, S, D = x.shape
    n_t = S // TM_IN
    tok = lambda w: pl.BlockSpec((None, TM_IN, w), lambda b, i: (b, i, 0))
    tokt = lambda r: pl.BlockSpec((None, r, TM_IN), lambda b, i: (b, 0, i))
    consts = (g1, wnat, wt, conv, bias16, ucum, ubd, e48)
    hbm = pl.BlockSpec(memory_space=pl.ANY)
    out_shape = (
        jax.ShapeDtypeStruct((B, S, D), F32),
        jax.ShapeDtypeStruct((B, S, NH_M * LANES), BF16),
        jax.ShapeDtypeStruct((B, S, NH_M * LANES), BF16),
        jax.ShapeDtypeStruct((B, S, D_MV), BF16),
        jax.ShapeDtypeStruct((B, D_MV, S), BF16),
        jax.ShapeDtypeStruct((B, 2 * NH_M, S), F32),
        jax.ShapeDtypeStruct((B, S, NH_F * LANES), BF16),
        jax.ShapeDtypeStruct((B, S, NH_F * LANES), BF16),
        jax.ShapeDtypeStruct((B, D_FV, S), BF16),
    )
    out_specs = (tok(D), tok(NH_M * LANES), tok(NH_M * LANES), tok(D_MV), tokt(D_MV), tokt(2 * NH_M),
                 tok(NH_F * LANES), tok(NH_F * LANES), tokt(D_FV))
    return pl.pallas_call(
        _mix_in_kernel,
        grid=(B, n_t),
        in_specs=[tok(D), hbm, hbm, hbm] + [_const_spec(c.shape) for c in consts],
        out_specs=out_specs,
        out_shape=out_shape,
        scratch_shapes=[pltpu.VMEM((TM_IN + SUBLANES, D_MQK), F32), pltpu.VMEM((N_GATES, LANES), F32),
                        pltpu.VMEM((D, D_FF), BF16), pltpu.VMEM((D, D_FF), BF16), pltpu.VMEM((D_FF, D), BF16),
                        pltpu.VMEM((2, W_STAGE_ROWS, D_FF), F32), pltpu.VMEM((2, W_STAGE_ROWS, D), F32),
                        pltpu.SemaphoreType.DMA((2,))],
        compiler_params=pltpu.CompilerParams(
            dimension_semantics=("arbitrary", "arbitrary"), vmem_limit_bytes=VMEM_LIMIT),
        name="mix_in",
    )(x, wg, wu, wd, *consts)


def _mlstm_kernel(q_ref, k_ref, vt_ref, mg_ref, mo_ref, o_ref, ct_ref):
    S = q_ref.shape[0]
    L = L_M
    ct_ref[...] = jnp.zeros(ct_ref.shape, F32)
    causal = (lax.broadcasted_iota(jnp.int32, (L, L), 0) <= lax.broadcasted_iota(jnp.int32, (L, L), 1))
    r16 = lax.broadcasted_iota(jnp.int32, (BF16_ROWS, L), 0)
    ones_rows = jnp.where(r16 < 3, 1.0, 0.0).astype(BF16)
    ones16 = jnp.ones((BF16_ROWS, L), BF16)

    def head_chunk(cs, h, ms):
        hs = slice(LANES * h, LANES * (h + 1))
        m = ms[h]
        logi = LOG2E * mg_ref[h:h + 1, pl.ds(cs, L)]
        b = LOG2E * mg_ref[NH_M + h:NH_M + h + 1, pl.ds(cs, L)]
        a = logi - b
        gtot = b[:, L - 1:L]
        m_new = jnp.maximum(gtot + m, gtot + jnp.max(a, axis=1, keepdims=True))
        ms[h] = m_new
        ahi, amid, alo = (p.astype(F32) for p in _split3(a))
        at = jnp.where(r16 == 0, ahi, jnp.where(r16 == 1, amid, jnp.where(r16 == 2, alo, 0.0)))
        acol = lax.dot_general(at.astype(BF16), ones_rows, TN, preferred_element_type=F32)
        q = q_ref[pl.ds(cs, L), hs]
        k = k_ref[pl.ds(cs, L), hs]
        st = lax.dot_general(k, q, NT, preferred_element_type=F32)
        ct = ct_ref[h]
        inter = lax.dot_general(ct.astype(BF16), q, NT, preferred_element_type=F32)
        yield
        e = jnp.where(causal, acol, -jnp.inf)
        mu = jnp.maximum(m, jnp.max(e, axis=0, keepdims=True))
        pt = (jnp.exp2(e - mu) * st).astype(BF16)
        vt_aug = jnp.concatenate([vt_ref[hs, pl.ds(cs, L)], ones16], axis=0)
        w = jnp.exp2(a + gtot - m_new)
        vtw = vt_aug * w.astype(BF16)
        yield
        nd = jnp.exp2(m - mu) * inter + _dot(vt_aug, pt)
        ct_ref[h] = jnp.exp2(gtot + m - m_new) * ct + _dot(vtw, k)
        yield
        rden = 1.0 / jnp.maximum(jnp.abs(nd[DV_M:DV_M + 1]), jnp.exp2(-(b + mu)))
        ht = nd[0:DV_M] * rden
        yt = ht * lax.rsqrt(jnp.mean(ht * ht, axis=0, keepdims=True) + EPS)
        og = jax.nn.sigmoid(mo_ref[pl.ds(cs, L), hs].astype(F32))
        o_ref[pl.ds(cs, L), hs] = (yt.T * og).astype(o_ref.dtype)
        yield

    def trip(c, ms):
        ms = list(ms)
        starts = [pl.multiple_of((c * MLSTM_CHUNKS + j) * L, L) for j in range(MLSTM_CHUNKS)]
        _skew([head_chunk(cs, h, ms) for cs in starts for h in range(NH_M)], lag=1, early=0)
        return tuple(ms)

    lax.fori_loop(0, S // (L * MLSTM_CHUNKS), trip, tuple(jnp.zeros((1, 1), F32) for _ in range(NH_M)))


def _mlstm(mq, mk, mvt, mg, mo):
    B, S, _ = mq.shape
    tok = pl.BlockSpec((None, S, NH_M * LANES), lambda b: (b, 0, 0))
    return pl.pallas_call(
        _mlstm_kernel,
        grid=(B,),
        in_specs=[tok, tok,
                  pl.BlockSpec((None, D_MV, S), lambda b: (b, 0, 0)),
                  pl.BlockSpec((None, 2 * NH_M, S), lambda b: (b, 0, 0)),
                  pl.BlockSpec((None, S, D_MV), lambda b: (b, 0, 0))],
        out_specs=pl.BlockSpec((None, S, D_MV), lambda b: (b, 0, 0)),
        out_shape=jax.ShapeDtypeStruct((B, S, D_MV), BF16),
        scratch_shapes=[pltpu.VMEM((NH_M, DV_M + BF16_ROWS, LANES), F32)],
        compiler_params=pltpu.CompilerParams(
            dimension_semantics=("arbitrary",), vmem_limit_bytes=VMEM_LIMIT),
        name="mlstm",
    )(mq, mk, mvt, mg, mo)


def _fox_kernel(q_ref, k_ref, vt_ref, o_ref, st_a, st_b, mb_a, mb_b, m_ref, acc_ref, bias_ref):
    S = q_ref.shape[0]
    T = T_F
    n_q = S // T
    heads = [slice(LANES * j, LANES * (j + 1)) for j in range(FOX_HEADS)]
    bufs = ((st_a, mb_a), (st_b, mb_b))
    n_strips = T // MXU_COLS
    parts = [(j, hc) for j in range(FOX_HEADS) for hc in range(n_strips)]
    ones16 = jnp.ones((BF16_ROWS, T), BF16)
    for hc in range(n_strips):
        visible = (lax.broadcasted_iota(jnp.int32, (T, MXU_COLS), 0)
                   <= lax.broadcasted_iota(jnp.int32, (T, MXU_COLS), 1) + hc * MXU_COLS)
        bias_ref[hc] = jnp.where(visible, 0.0, NEG)

    def n_keys(diag, hc):
        return (hc + 1) * MXU_COLS if diag else T

    def logits(c, buf, diag, part):
        qi, kj = c
        st_buf, mb_buf = buf
        j, hc = part
        nk = n_keys(diag, hc)
        qs = pl.multiple_of(qi * T + hc * MXU_COLS, MXU_COLS)
        ks = pl.multiple_of(kj * T, T)
        st = lax.dot_general(k_ref[pl.ds(ks, nk), heads[j]], q_ref[pl.ds(qs, MXU_COLS), heads[j]], NT,
                             preferred_element_type=F32)
        if diag:
            st = st + bias_ref[hc, 0:nk, :]
        st_buf[j, hc, 0:nk, :] = st
        mb_buf[j, hc] = jnp.max(st, axis=0, keepdims=True)

    def update(c, buf, first, part):
        qi, kj = c
        st_buf, mb_buf = buf
        j, hc = part
        nk = n_keys(first, hc)
        ks = pl.multiple_of(kj * T, T)
        m_old = m_ref[qi, j, hc]
        m_new = mb_buf[j, hc] if first else jnp.maximum(m_old, mb_buf[j, hc])
        pt = jnp.exp2(st_buf[j, hc, 0:nk, :] - m_new).astype(BF16)
        vt_aug = jnp.concatenate([vt_ref[DH_F * j:DH_F * (j + 1), pl.ds(ks, nk)], ones16[:, 0:nk]], axis=0)
        pv = _dot(vt_aug, pt)
        acc_ref[qi, j, hc] = pv if first else jnp.exp2(m_old - m_new) * acc_ref[qi, j, hc] + pv
        m_ref[qi, j, hc] = m_new

    def step(cur, nxt, b, diag):
        for part in parts:
            if nxt is not None:
                logits(nxt, bufs[1 - b], diag, part)
            update(cur, bufs[b], diag, part)

    def sweep(n, start, succ, diag):
        if n == 0:
            return
        for part in parts:
            logits(start, bufs[0], diag, part)

        def trip(_, c):
            for i in range(FOX_UNROLL):
                cn = succ(c)
                step(c, cn, i % 2, diag)
                c = cn
            return c

        trips = (n - 1) // FOX_UNROLL
        c = lax.fori_loop(0, trips, trip, start)
        rest = n - trips * FOX_UNROLL
        for i in range(rest):
            cn = succ(c) if i < rest - 1 else None
            step(c, cn, i % 2, diag)
            c = cn

    zero = jnp.int32(0)
    sweep(n_q, (zero, zero), lambda c: (c[0] + 1, c[1] + 1), True)

    def below(c):
        wrap = c[1] + 1 >= c[0]
        return jnp.where(wrap, c[0] + 1, c[0]), jnp.where(wrap, 0, c[1] + 1)

    sweep(n_q * (n_q - 1) // 2, (zero + 1, zero), below, False)

    def finish(qi, _):
        ys = []
        for j in range(FOX_HEADS):
            acc = jnp.concatenate([acc_ref[qi, j, hc] for hc in range(n_strips)], axis=1)
            o = acc[0:DH_F] * (1.0 / acc[DH_F:DH_F + 1])
            ys.append(o * lax.rsqrt(jnp.mean(o * o, axis=0, keepdims=True) + EPS))
        o_ref[pl.ds(pl.multiple_of(qi * T, T), T), :] = jnp.concatenate(ys, axis=0).T.astype(o_ref.dtype)
        return 0

    lax.fori_loop(0, n_q, finish, 0, unroll=FOX_FINISH_UNROLL)


def _fox(fq, fk, fvt):
    B, S, _ = fq.shape
    n_pair = NH_F // FOX_HEADS
    n_strips = T_F // MXU_COLS
    tok = pl.BlockSpec((None, S, FOX_HEADS * LANES), lambda b, p: (b, 0, p))
    return pl.pallas_call(
        _fox_kernel,
        grid=(B, n_pair),
        in_specs=[tok, tok,
                  pl.BlockSpec((None, FOX_HEADS * DH_F, S), lambda b, p: (b, p, 0))],
        out_specs=pl.BlockSpec((None, S, FOX_HEADS * DH_F), lambda b, p: (b, 0, p)),
        out_shape=jax.ShapeDtypeStruct((B, S, D_FV), BF16),
        scratch_shapes=[pltpu.VMEM((FOX_HEADS, n_strips, T_F, MXU_COLS), F32), pltpu.VMEM((FOX_HEADS, n_strips, T_F, MXU_COLS), F32),
                        pltpu.VMEM((FOX_HEADS, n_strips, 1, MXU_COLS), F32), pltpu.VMEM((FOX_HEADS, n_strips, 1, MXU_COLS), F32),
                        pltpu.VMEM((S // T_F, FOX_HEADS, n_strips, 1, MXU_COLS), F32),
                        pltpu.VMEM((S // T_F, FOX_HEADS, n_strips, DH_F + BF16_ROWS, MXU_COLS), F32),
                        pltpu.VMEM((n_strips, T_F, MXU_COLS), F32)],
        compiler_params=pltpu.CompilerParams(
            dimension_semantics=("arbitrary", "arbitrary"), vmem_limit_bytes=VMEM_LIMIT),
        name="fox",
    )(fq, fk, fvt)


def _mix_out_kernel(h_ref, ym_ref, yf_ref, p_ref, wom_ref, wof_ref, wg_ref, wu_ref, wd_ref,
                    wpg_ref, wpp_ref, npp_ref, nf_ref, o_ref):
    def unit(i):
        rows = slice(i * UNIT, (i + 1) * UNIT)
        h = h_ref[rows, :] + _dot(ym_ref[rows, :], wom_ref[...]) + _dot(yf_ref[rows, :], wof_ref[...])
        pe = _rms(_dot(p_ref[rows, :].astype(BF16), wpp_ref[...]), npp_ref[...])
        xn = _unit_rms(h).astype(BF16)
        yield
        acc = None
        for c in range(0, D_FF, FF_CHUNK):
            part = _swiglu_chunk(xn, c, wg_ref, wu_ref, wd_ref)
            acc = part if acc is None else acc + part
            yield
        h = h + acc
        xg = _unit_rms(h).astype(BF16)
        yield
        h = h + jax.nn.sigmoid(_dot(xg, wpg_ref[...])) * pe
        o_ref[rows, :] = _rms(h, nf_ref[...])
        yield

    _skew([unit(i) for i in range(h_ref.shape[0] // UNIT)], lag=N_FF_CHUNKS + 1, early=N_FF_CHUNKS - 2)


def _mix_out(h1, ym, yf, p, *consts):
    B, S, D = h1.shape
    tok = lambda w: pl.BlockSpec((None, TM_OUT, w), lambda b, i: (b, i, 0))
    return pl.pallas_call(
        _mix_out_kernel,
        grid=(B, S // TM_OUT),
        in_specs=[tok(D), tok(D_MV), tok(D_FV), tok(p.shape[-1])] + [_const_spec(c.shape) for c in consts],
        out_specs=tok(D),
        out_shape=jax.ShapeDtypeStruct((B, S, D), F32),
        compiler_params=pltpu.CompilerParams(
            dimension_semantics=("arbitrary", "arbitrary"), vmem_limit_bytes=VMEM_LIMIT),
        name="mix_out",
    )(h1, ym, yf, p, *consts)


_O_MQK = 0
_O_MV = _O_MQK + D_MQK
_O_MO = _O_MV + D_MV
_O_MIF = _O_MO + D_MV
_O_FQ = _O_MIF + 2 * NH_M
_O_FK = _O_FQ + D_FV
_O_FV = _O_FK + D_FV
_O_FF = _O_FV + D_FV
_NAT_COLS = np.concatenate([np.arange(_O_MQK, _O_MQK + D_MQK), np.arange(_O_MO, _O_MO + D_MV),
                            np.arange(_O_FQ, _O_FQ + D_FV), np.arange(_O_FK, _O_FK + D_FV)])
_T_COLS = np.concatenate([np.arange(_O_MV, _O_MV + D_MV), np.arange(_O_FV, _O_FV + D_FV),
                          np.arange(_O_MIF, _O_MIF + 2 * NH_M), np.arange(_O_FF, _O_FF + NH_F)])


def _placement():
    e = np.zeros((3 * N_GATES, NH_F * LANES), np.float32)
    for j in range(3):
        for h in range(NH_F):
            e[N_GATES * j + 2 * NH_M + h, LANES * h + (j if h % 2 else DH_F + j)] = 1.0
    return e


def _tri(n, block):
    s = np.arange(n)[:, None]
    t = np.arange(n)[None, :]
    return ((s <= t) & (s // block == t // block)).astype(np.float32)


def _take_cols(w, cols):
    cuts = [0] + [i for i in range(1, len(cols)) if cols[i] != cols[i - 1] + 1] + [len(cols)]
    return jnp.concatenate([w[:, int(cols[a]):int(cols[b - 1]) + 1] for a, b in zip(cuts[:-1], cuts[1:])], axis=1)


def kernel(x, p, ffn1_norm, ffn1_w_gate, ffn1_w_up, ffn1_w_down, mix_norm, w_in, conv_qk, b_mlstm_gates, b_fox_f, mlstm_out_norm, fox_out_norm, w_out, ffn2_norm, ffn2_w_gate, ffn2_w_up, ffn2_w_down, ple_gate_norm, w_ple_gate, w_ple_proj, ple_proj_norm, final_norm):
    depth = w_in.shape[0]
    assert depth == 1, "the final RMSNorm is fused into the (single) layer's last kernel"
    assert x.shape[1] % TM_OUT == 0 and x.shape[1] % (L_M * MLSTM_CHUNKS) == 0 and x.shape[2] == D_MODEL
    assert TM_IN % UNIT == 0 and TM_OUT % UNIT == 0 and UNIT % L_M == 0 and FOX_UNROLL % 2 == 0
    row = lambda v: v.reshape(1, -1).astype(F32)
    col = lambda v: jnp.broadcast_to(v.astype(F32)[:, None], (v.shape[0], LANES))
    ucum = jnp.asarray(_tri(UNIT, UNIT), BF16)
    ubd = jnp.asarray(_tri(UNIT, L_M), BF16)
    e48 = jnp.asarray(_placement(), BF16)
    h = x
    for i in range(depth):
        def scaled(w, g):
            return (g.astype(F32)[:, None] * w).astype(BF16)

        wnat = scaled(_take_cols(w_in[i], _NAT_COLS), mix_norm[i])
        wt = scaled(_take_cols(w_in[i], _T_COLS), mix_norm[i]).T
        conv = conv_qk[i].astype(F32)
        bias16 = col(jnp.concatenate([b_mlstm_gates[i], b_fox_f[i]]))
        h1, mq, mk, mo, mvt, mg, fq, fk, fvt = _mix_in(
            h, ffn1_w_gate[i], ffn1_w_up[i], ffn1_w_down[i], col(ffn1_norm[i]), wnat, wt, conv, bias16, ucum, ubd, e48)
        ym = _mlstm(mq, mk, mvt, mg, mo)
        yf = _fox(fq, fk, fvt)
        wo = scaled(w_out[i], jnp.concatenate([mlstm_out_norm[i], fox_out_norm[i]]))
        h = _mix_out(
            h1, ym, yf, p[i], wo[:D_MV], wo[D_MV:], scaled(ffn2_w_gate[i], ffn2_norm[i]),
            scaled(ffn2_w_up[i], ffn2_norm[i]), (0.5 * ffn2_w_down[i]).astype(BF16),
            scaled(w_ple_gate[i], ple_gate_norm[i]), w_ple_proj[i].astype(BF16), row(ple_proj_norm[i]),
            row(final_norm))
    return h
```

```python
import numpy as np
import jax
import jax.numpy as jnp
from jax import lax
from jax.experimental import pallas as pl
from jax.experimental.pallas import tpu as pltpu

F32 = jnp.float32
BF16 = jnp.bfloat16

EPS = 1e-6
D_MODEL = 1024
D_FF = 2816
NH_M = 4
DV_M = 128
DK_M = 64
CONV_W = 4
NH_F = 8
DH_F = 64
D_MV = NH_M * DV_M
D_FV = NH_F * DH_F
D_MQK = 2 * NH_M * DK_M
N_GATES = 2 * NH_M + NH_F

LANES = 128
SUBLANES = 8
BF16_ROWS = 16
MXU_COLS = 256
TM_IN = 512
TM_OUT = 1024
UNIT = 512
W_STAGE_ROWS = 256
FF_CHUNK = 256
N_FF_CHUNKS = D_FF // FF_CHUNK
L_M = 256
MLSTM_CHUNKS = 8
T_F = 512
LOG2E = 1.4426950408889634
FOX_HEADS = 4
FOX_UNROLL = 8
FOX_FINISH_UNROLL = 4
NEG = -1e30
VMEM_LIMIT = 60 * 1024 * 1024

NT = (((1,), (1,)), ((), ()))
TN = (((0,), (0,)), ((), ()))


def _unit_rms(x):
    return x * lax.rsqrt(jnp.mean(x * x, axis=-1, keepdims=True) + EPS)


def _rms(x, g):
    return _unit_rms(x) * g


def _dot(a, b):
    return jnp.dot(a, b, preferred_element_type=F32)


def _split3(x):
    hi = x.astype(BF16)
    r = x - hi.astype(F32)
    mid = r.astype(BF16)
    lo = (r - mid.astype(F32)).astype(BF16)
    return hi, mid, lo


def _swiglu_chunk(xn, c, wg_ref, wu_ref, wd_ref):
    a = _dot(xn, wg_ref[:, c:c + FF_CHUNK])
    b = _dot(xn, wu_ref[:, c:c + FF_CHUNK])
    return _dot((a * jax.nn.sigmoid(a) * b).astype(BF16), wd_ref[c:c + FF_CHUNK, :])


def _skew(units, lag, early):
    done = [0] * len(units)
    tick = 0
    while any(d >= 0 for d in done):
        for i, u in enumerate(units):
            due = max(0, i * lag - early) if done[i] == 0 else i * lag + done[i]
            if done[i] >= 0 and tick >= due:
                try:
                    next(u)
                    done[i] += 1
                except StopIteration:
                    done[i] = -1
        tick += 1


def _stage_ffn_weights(jobs, stage, sem):
    def copy(j):
        src, r, _ = jobs[j]
        buf = stage[src.shape[1]]
        return pltpu.make_async_copy(src.at[pl.ds(r, W_STAGE_ROWS), :], buf.at[j % 2], sem.at[j % 2])

    copy(0).start()
    for j, (src, r, convert) in enumerate(jobs):
        if j + 1 < len(jobs):
            copy(j + 1).start()
        copy(j).wait()
        convert(stage[src.shape[1]][j % 2])


def _mix_in_kernel(x_ref, wg_hbm, wu_hbm, wd_hbm, g1_ref, wnat_ref, wt_ref, conv_ref,
                   bias_ref, ucum_ref, ubd_ref, e48_ref,
                   h1_ref, mq_ref, mk_ref, mo_ref, mvt_ref, mg_ref, fq_ref, fk_ref, fvt_ref,
                   cbuf, carry, wg_ref, wu_ref, wd_ref, stage_ff, stage_dm, sem):
    tm = x_ref.shape[0]
    n_units = tm // UNIT

    @pl.when((pl.program_id(0) == 0) & (pl.program_id(1) == 0))
    def _():
        def up(dst, r):
            def convert(blk):
                g = jnp.tile(g1_ref[r:r + W_STAGE_ROWS, :], (1, D_FF // LANES))
                dst[r:r + W_STAGE_ROWS, :] = (g * blk).astype(BF16)
            return convert

        def down(r):
            def convert(blk):
                wd_ref[r:r + W_STAGE_ROWS, :] = (0.5 * blk).astype(BF16)
            return convert

        jobs = ([(wg_hbm, r, up(wg_ref, r)) for r in range(0, D_MODEL, W_STAGE_ROWS)]
                + [(wu_hbm, r, up(wu_ref, r)) for r in range(0, D_MODEL, W_STAGE_ROWS)]
                + [(wd_hbm, r, down(r)) for r in range(0, D_FF, W_STAGE_ROWS)])
        _stage_ffn_weights(jobs, {D_FF: stage_ff, D_MODEL: stage_dm}, sem)

    @pl.when(pl.program_id(1) == 0)
    def _():
        cbuf[0:SUBLANES, :] = jnp.zeros((SUBLANES, cbuf.shape[1]), F32)
        carry[...] = jnp.zeros(carry.shape, F32)

    lane = lax.broadcasted_iota(jnp.int32, (UNIT, LANES), 1)
    low = lane < DK_M
    ones_lo = jnp.where((lane >= DH_F) & (lane < DH_F + 3), 1.0, 0.0)
    ones_hi = jnp.where(lane < 3, 1.0, 0.0)
    r8 = lax.broadcasted_iota(jnp.int32, (2 * NH_M, UNIT), 0)

    def groups(z, fill):
        out = []
        for j in range(z.shape[1] // LANES):
            c = z[:, LANES * j:LANES * (j + 1)]
            out += [jnp.where(low, c, fill(2 * j)), jnp.where(low, fill(2 * j + 1), c)]
        return out

    def store_groups(ref, rows, gs):
        for g, v in enumerate(gs):
            ref[rows, LANES * g:LANES * (g + 1)] = v.astype(ref.dtype)

    def fold(r):
        return r[0:N_GATES] + r[N_GATES:2 * N_GATES] + r[2 * N_GATES:3 * N_GATES]

    zero = lambda j: 0.0

    def unit(i):
        r0 = i * UNIT
        rows = slice(r0, r0 + UNIT)
        x = x_ref[rows, :]
        xn = _unit_rms(x).astype(BF16)
        yield
        acc = None
        for c in range(0, D_FF, FF_CHUNK):
            part = _swiglu_chunk(xn, c, wg_ref, wu_ref, wd_ref)
            acc = part if acc is None else acc + part
            yield
        h1 = x + acc
        h1_ref[rows, :] = h1
        u = _unit_rms(h1).astype(BF16)
        yield

        zqk = _dot(u, wnat_ref[:, 0:D_MQK])
        cbuf[SUBLANES + r0:SUBLANES + r0 + UNIT, :] = zqk
        yield
        cv = conv_ref[CONV_W - 1:CONV_W, :] * zqk
        for j in range(CONV_W - 1):
            top = SUBLANES + r0 - (CONV_W - 1) + j
            cv = cv + conv_ref[j:j + 1, :] * cbuf[top:top + UNIT, :]
        if i == n_units - 1:
            cbuf[0:SUBLANES, :] = cbuf[tm:tm + SUBLANES, :]
        s = cv * jax.nn.sigmoid(cv)
        store_groups(mq_ref, rows, groups(s[:, 0:D_MQK // 2] * (DK_M ** -0.5), zero))
        store_groups(mk_ref, rows, groups(s[:, D_MQK // 2:D_MQK], zero))
        yield

        mo_ref[rows, :] = _dot(u, wnat_ref[:, D_MQK:D_MQK + D_MV]).astype(BF16)
        yield

        zt = lax.dot_general(wt_ref[...], u, NT, preferred_element_type=F32)
        mvt_ref[:, rows] = zt[0:D_MV].astype(BF16)
        fvt_ref[:, rows] = zt[D_MV:D_MV + D_FV].astype(BF16)
        g16 = zt[D_MV + D_FV:D_MV + D_FV + N_GATES] + jnp.tile(bias_ref[...], (1, UNIT // LANES))
        yield
        lf = jnp.minimum(g16, 0.0) - jnp.log1p(jnp.exp(-jnp.abs(g16)))
        xs = jnp.concatenate(_split3(lf), axis=0)
        cum = fold(_dot(xs, ucum_ref[...]))
        bch = fold(_dot(xs, ubd_ref[...]))
        tot = fold(_dot(xs, jnp.ones((UNIT, LANES), BF16)))
        c16 = cum + jnp.tile(carry[...], (1, UNIT // LANES))
        carry[...] = carry[...] + tot
        mg_ref[:, rows] = jnp.where(r8 < NH_M, g16[0:2 * NH_M], bch[0:2 * NH_M])
        ys = jnp.concatenate(_split3(-LOG2E * c16), axis=0)
        place = lax.dot_general(ys, e48_ref[...], TN, preferred_element_type=F32)
        yield
        zfq = _dot(u, wnat_ref[:, D_MQK + D_MV:D_MQK + D_MV + D_FV]) * (DH_F ** -0.5 * LOG2E)
        store_groups(fq_ref, rows, groups(zfq, lambda g: ones_hi if g % 2 else ones_lo))
        yield
        zfk = _dot(u, wnat_ref[:, D_MQK + D_MV + D_FV:D_MQK + D_MV + 2 * D_FV])
        store_groups(fk_ref, rows, groups(zfk, lambda g: place[:, LANES * g:LANES * (g + 1)]))
        yield

    _skew([unit(i) for i in range(n_units)], lag=N_FF_CHUNKS + 1, early=N_FF_CHUNKS - 2)


def _const_spec(shape):
    nd = len(shape)
    return pl.BlockSpec(shape, lambda *_: (0,) * nd, pipeline_mode=pl.Buffered(1))


def _mix_in(x, wg, wu, wd, g1, wnat, wt, conv, bias16, ucum, ubd, e48):
    B, S, D = x.shape
    n_t = S // TM_IN
    tok = lambda w: pl.BlockSpec((None, TM_IN, w), lambda b, i: (b, i, 0))
    tokt = lambda r: pl.BlockSpec((None, r, TM_IN), lambda b, i: (b, 0, i))
    consts = (g1, wnat, wt, conv, bias16, ucum, ubd, e48)
    hbm = pl.BlockSpec(memory_space=pl.ANY)
    out_shape = (
        jax.ShapeDtypeStruct((B, S, D), F32),
        jax.ShapeDtypeStruct((B, S, NH_M * LANES), BF16),
        jax.ShapeDtypeStruct((B, S, NH_M * LANES), BF16),
        jax.ShapeDtypeStruct((B, S, D_MV), BF16),
        jax.ShapeDtypeStruct((B, D_MV, S), BF16),
        jax.ShapeDtypeStruct((B, 2 * NH_M, S), F32),
        jax.ShapeDtypeStruct((B, S, NH_F * LANES), BF16),
        jax.ShapeDtypeStruct((B, S, NH_F * LANES), BF16),
        jax.ShapeDtypeStruct((B, D_FV, S), BF16),
    )
    out_specs = (tok(D), tok(NH_M * LANES), tok(NH_M * LANES), tok(D_MV), tokt(D_MV), tokt(2 * NH_M),
                 tok(NH_F * LANES), tok(NH_F * LANES), tokt(D_FV))
    return pl.pallas_call(
        _mix_in_kernel,
        grid=(B, n_t),
        in_specs=[tok(D), hbm, hbm, hbm] + [_const_spec(c.shape) for c in consts],
        out_specs=out_specs,
        out_shape=out_shape,
        scratch_shapes=[pltpu.VMEM((TM_IN + SUBLANES, D_MQK), F32), pltpu.VMEM((N_GATES, LANES), F32),
                        pltpu.VMEM((D, D_FF), BF16), pltpu.VMEM((D, D_FF), BF16), pltpu.VMEM((D_FF, D), BF16),
                        pltpu.VMEM((2, W_STAGE_ROWS, D_FF), F32), pltpu.VMEM((2, W_STAGE_ROWS, D), F32),
                        pltpu.SemaphoreType.DMA((2,))],
        compiler_params=pltpu.CompilerParams(
            dimension_semantics=("arbitrary", "arbitrary"), vmem_limit_bytes=VMEM_LIMIT),
        name="mix_in",
    )(x, wg, wu, wd, *consts)


def _mlstm_kernel(q_ref, k_ref, vt_ref, mg_ref, mo_ref, o_ref, ct_ref):
    S = q_ref.shape[0]
    L = L_M
    ct_ref[...] = jnp.zeros(ct_ref.shape, F32)
    causal = (lax.broadcasted_iota(jnp.int32, (L, L), 0) <= lax.broadcasted_iota(jnp.int32, (L, L), 1))
    r16 = lax.broadcasted_iota(jnp.int32, (BF16_ROWS, L), 0)
    ones_rows = jnp.where(r16 < 3, 1.0, 0.0).astype(BF16)
    ones16 = jnp.ones((BF16_ROWS, L), BF16)

    def head_chunk(cs, h, ms):
        hs = slice(LANES * h, LANES * (h + 1))
        m = ms[h]
        logi = LOG2E * mg_ref[h:h + 1, pl.ds(cs, L)]
        b = LOG2E * mg_ref[NH_M + h:NH_M + h + 1, pl.ds(cs, L)]
        a = logi - b
        gtot = b[:, L - 1:L]
        m_new = jnp.maximum(gtot + m, gtot + jnp.max(a, axis=1, keepdims=True))
        ms[h] = m_new
        ahi, amid, alo = (p.astype(F32) for p in _split3(a))
        at = jnp.where(r16 == 0, ahi, jnp.where(r16 == 1, amid, jnp.where(r16 == 2, alo, 0.0)))
        acol = lax.dot_general(at.astype(BF16), ones_rows, TN, preferred_element_type=F32)
        q = q_ref[pl.ds(cs, L), hs]
        k = k_ref[pl.ds(cs, L), hs]
        st = lax.dot_general(k, q, NT, preferred_element_type=F32)
        ct = ct_ref[h]
        inter = lax.dot_general(ct.astype(BF16), q, NT, preferred_element_type=F32)
        yield
        e = jnp.where(causal, acol, -jnp.inf)
        mu = jnp.maximum(m, jnp.max(e, axis=0, keepdims=True))
        pt = (jnp.exp2(e - mu) * st).astype(BF16)
        vt_aug = jnp.concatenate([vt_ref[hs, pl.ds(cs, L)], ones16], axis=0)
        w = jnp.exp2(a + gtot - m_new)
        vtw = vt_aug * w.astype(BF16)
        yield
        nd = jnp.exp2(m - mu) * inter + _dot(vt_aug, pt)
        ct_ref[h] = jnp.exp2(gtot + m - m_new) * ct + _dot(vtw, k)
        yield
        rden = 1.0 / jnp.maximum(jnp.abs(nd[DV_M:DV_M + 1]), jnp.exp2(-(b + mu)))
        ht = nd[0:DV_M] * rden
        yt = ht * lax.rsqrt(jnp.mean(ht * ht, axis=0, keepdims=True) + EPS)
        og = jax.nn.sigmoid(mo_ref[pl.ds(cs, L), hs].astype(F32))
        o_ref[pl.ds(cs, L), hs] = (yt.T * og).astype(o_ref.dtype)
        yield

    def trip(c, ms):
        ms = list(ms)
        starts = [pl.multiple_of((c * MLSTM_CHUNKS + j) * L, L) for j in range(MLSTM_CHUNKS)]
        _skew([head_chunk(cs, h, ms) for cs in starts for h in range(NH_M)], lag=1, early=0)
        return tuple(ms)

    lax.fori_loop(0, S // (L * MLSTM_CHUNKS), trip, tuple(jnp.zeros((1, 1), F32) for _ in range(NH_M)))


def _mlstm(mq, mk, mvt, mg, mo):
    B, S, _ = mq.shape
    tok = pl.BlockSpec((None, S, NH_M * LANES), lambda b: (b, 0, 0))
    return pl.pallas_call(
        _mlstm_kernel,
        grid=(B,),
        in_specs=[tok, tok,
                  pl.BlockSpec((None, D_MV, S), lambda b: (b, 0, 0)),
                  pl.BlockSpec((None, 2 * NH_M, S), lambda b: (b, 0, 0)),
                  pl.BlockSpec((None, S, D_MV), lambda b: (b, 0, 0))],
        out_specs=pl.BlockSpec((None, S, D_MV), lambda b: (b, 0, 0)),
        out_shape=jax.ShapeDtypeStruct((B, S, D_MV), BF16),
        scratch_shapes=[pltpu.VMEM((NH_M, DV_M + BF16_ROWS, LANES), F32)],
        compiler_params=pltpu.CompilerParams(
            dimension_semantics=("arbitrary",), vmem_limit_bytes=VMEM_LIMIT),
        name="mlstm",
    )(mq, mk, mvt, mg, mo)


def _fox_kernel(q_ref, k_ref, vt_ref, o_ref, st_a, st_b, mb_a, mb_b, m_ref, acc_ref, bias_ref):
    S = q_ref.shape[0]
    T = T_F
    n_q = S // T
    heads = [slice(LANES * j, LANES * (j + 1)) for j in range(FOX_HEADS)]
    bufs = ((st_a, mb_a), (st_b, mb_b))
    n_strips = T // MXU_COLS
    parts = [(j, hc) for j in range(FOX_HEADS) for hc in range(n_strips)]
    ones16 = jnp.ones((BF16_ROWS, T), BF16)
    @pl.when((pl.program_id(0) == 0) & (pl.program_id(1) == 0))
    def _():
        for hc in range(n_strips):
            visible = (lax.broadcasted_iota(jnp.int32, (T, MXU_COLS), 0)
                       <= lax.broadcasted_iota(jnp.int32, (T, MXU_COLS), 1) + hc * MXU_COLS)
            bias_ref[hc] = jnp.where(visible, 0.0, NEG)

    def n_keys(diag, hc):
        return (hc + 1) * MXU_COLS if diag else T

    def logits(c, buf, diag, part):
        qi, kj = c
        st_buf, mb_buf = buf
        j, hc = part
        nk = n_keys(diag, hc)
        qs = pl.multiple_of(qi * T + hc * MXU_COLS, MXU_COLS)
        ks = pl.multiple_of(kj * T, T)
        st = lax.dot_general(k_ref[pl.ds(ks, nk), heads[j]], q_ref[pl.ds(qs, MXU_COLS), heads[j]], NT,
                             preferred_element_type=F32)
        if diag:
            st = st + bias_ref[hc, 0:nk, :]
        st_buf[j, hc, 0:nk, :] = st
        mb_buf[j, hc] = jnp.max(st, axis=0, keepdims=True)

    def update(c, buf, first, part):
        qi, kj = c
        st_buf, mb_buf = buf
        j, hc = part
        nk = n_keys(first, hc)
        ks = pl.multiple_of(kj * T, T)
        m_old = m_ref[qi, j, hc]
        m_new = mb_buf[j, hc] if first else jnp.maximum(m_old, mb_buf[j, hc])
        pt = jnp.exp2(st_buf[j, hc, 0:nk, :] - m_new).astype(BF16)
        vt_aug = jnp.concatenate([vt_ref[DH_F * j:DH_F * (j + 1), pl.ds(ks, nk)], ones16[:, 0:nk]], axis=0)
        pv = _dot(vt_aug, pt)
        acc_ref[qi, j, hc] = pv if first else jnp.exp2(m_old - m_new) * acc_ref[qi, j, hc] + pv
        m_ref[qi, j, hc] = m_new

    def step(cur, nxt, b, diag):
        for part in parts:
            if nxt is not None:
                logits(nxt, bufs[1 - b], diag, part)
            update(cur, bufs[b], diag, part)

    def sweep(n, start, succ, diag):
        if n == 0:
            return
        for part in parts:
            logits(start, bufs[0], diag, part)

        def trip(_, c):
            for i in range(FOX_UNROLL):
                cn = succ(c)
                step(c, cn, i % 2, diag)
                c = cn
            return c

        trips = (n - 1) // FOX_UNROLL
        c = lax.fori_loop(0, trips, trip, start)
        rest = n - trips * FOX_UNROLL
        for i in range(rest):
            cn = succ(c) if i < rest - 1 else None
            step(c, cn, i % 2, diag)
            c = cn

    zero = jnp.int32(0)
    sweep(n_q, (zero, zero), lambda c: (c[0] + 1, c[1] + 1), True)

    def below(c):
        wrap = c[1] + 1 >= c[0]
        return jnp.where(wrap, c[0] + 1, c[0]), jnp.where(wrap, 0, c[1] + 1)

    sweep(n_q * (n_q - 1) // 2, (zero + 1, zero), below, False)

    def finish(qi, _):
        ys = []
        for j in range(FOX_HEADS):
            acc = jnp.concatenate([acc_ref[qi, j, hc] for hc in range(n_strips)], axis=1)
            o = acc[0:DH_F] * (1.0 / acc[DH_F:DH_F + 1])
            ys.append(o * lax.rsqrt(jnp.mean(o * o, axis=0, keepdims=True) + EPS))
        o_ref[pl.ds(pl.multiple_of(qi * T, T), T), :] = jnp.concatenate(ys, axis=0).T.astype(o_ref.dtype)
        return 0

    lax.fori_loop(0, n_q, finish, 0, unroll=FOX_FINISH_UNROLL)


def _fox(fq, fk, fvt):
    B, S, _ = fq.shape
    n_pair = NH_F // FOX_HEADS
    n_strips = T_F // MXU_COLS
    tok = pl.BlockSpec((None, S, FOX_HEADS * LANES), lambda b, p: (b, 0, p))
    return pl.pallas_call(
        _fox_kernel,
        grid=(B, n_pair),
        in_specs=[tok, tok,
                  pl.BlockSpec((None, FOX_HEADS * DH_F, S), lambda b, p: (b, p, 0))],
        out_specs=pl.BlockSpec((None, S, FOX_HEADS * DH_F), lambda b, p: (b, 0, p)),
        out_shape=jax.ShapeDtypeStruct((B, S, D_FV), BF16),
        scratch_shapes=[pltpu.VMEM((FOX_HEADS, n_strips, T_F, MXU_COLS), F32), pltpu.VMEM((FOX_HEADS, n_strips, T_F, MXU_COLS), F32),
                        pltpu.VMEM((FOX_HEADS, n_strips, 1, MXU_COLS), F32), pltpu.VMEM((FOX_HEADS, n_strips, 1, MXU_COLS), F32),
                        pltpu.VMEM((S // T_F, FOX_HEADS, n_strips, 1, MXU_COLS), F32),
                        pltpu.VMEM((S // T_F, FOX_HEADS, n_strips, DH_F + BF16_ROWS, MXU_COLS), F32),
                        pltpu.VMEM((n_strips, T_F, MXU_COLS), F32)],
        compiler_params=pltpu.CompilerParams(
            dimension_semantics=("arbitrary", "arbitrary"), vmem_limit_bytes=VMEM_LIMIT),
        name="fox",
    )(fq, fk, fvt)


def _mix_out_kernel(h_ref, ym_ref, yf_ref, p_ref, wom_ref, wof_ref, wg_ref, wu_ref, wd_ref,
                    wpg_ref, wpp_ref, npp_ref, nf_ref, o_ref):
    def unit(i):
        rows = slice(i * UNIT, (i + 1) * UNIT)
        h = h_ref[rows, :] + _dot(ym_ref[rows, :], wom_ref[...]) + _dot(yf_ref[rows, :], wof_ref[...])
        pe = _rms(_dot(p_ref[rows, :].astype(BF16), wpp_ref[...]), npp_ref[...])
        xn = _unit_rms(h).astype(BF16)
        yield
        acc = None
        for c in range(0, D_FF, FF_CHUNK):
            part = _swiglu_chunk(xn, c, wg_ref, wu_ref, wd_ref)
            acc = part if acc is None else acc + part
            yield
        h = h + acc
        xg = _unit_rms(h).astype(BF16)
        yield
        h = h + jax.nn.sigmoid(_dot(xg, wpg_ref[...])) * pe
        o_ref[rows, :] = _rms(h, nf_ref[...])
        yield

    _skew([unit(i) for i in range(h_ref.shape[0] // UNIT)], lag=N_FF_CHUNKS + 1, early=N_FF_CHUNKS - 2)


def _mix_out(h1, ym, yf, p, *consts):
    B, S, D = h1.shape
    tok = lambda w: pl.BlockSpec((None, TM_OUT, w), lambda b, i: (b, i, 0))
    return pl.pallas_call(
        _mix_out_kernel,
        grid=(B, S // TM_OUT),
        in_specs=[tok(D), tok(D_MV), tok(D_FV), tok(p.shape[-1])] + [_const_spec(c.shape) for c in consts],
        out_specs=tok(D),
        out_shape=jax.ShapeDtypeStruct((B, S, D), F32),
        compiler_params=pltpu.CompilerParams(
            dimension_semantics=("arbitrary", "arbitrary"), vmem_limit_bytes=VMEM_LIMIT),
        name="mix_out",
    )(h1, ym, yf, p, *consts)


_O_MQK = 0
_O_MV = _O_MQK + D_MQK
_O_MO = _O_MV + D_MV
_O_MIF = _O_MO + D_MV
_O_FQ = _O_MIF + 2 * NH_M
_O_FK = _O_FQ + D_FV
_O_FV = _O_FK + D_FV
_O_FF = _O_FV + D_FV
_NAT_COLS = np.concatenate([np.arange(_O_MQK, _O_MQK + D_MQK), np.arange(_O_MO, _O_MO + D_MV),
                            np.arange(_O_FQ, _O_FQ + D_FV), np.arange(_O_FK, _O_FK + D_FV)])
_T_COLS = np.concatenate([np.arange(_O_MV, _O_MV + D_MV), np.arange(_O_FV, _O_FV + D_FV),
                          np.arange(_O_MIF, _O_MIF + 2 * NH_M), np.arange(_O_FF, _O_FF + NH_F)])


def _placement():
    e = np.zeros((3 * N_GATES, NH_F * LANES), np.float32)
    for j in range(3):
        for h in range(NH_F):
            e[N_GATES * j + 2 * NH_M + h, LANES * h + (j if h % 2 else DH_F + j)] = 1.0
    return e


def _tri(n, block):
    s = np.arange(n)[:, None]
    t = np.arange(n)[None, :]
    return ((s <= t) & (s // block == t // block)).astype(np.float32)


def _take_cols(w, cols):
    cuts = [0] + [i for i in range(1, len(cols)) if cols[i] != cols[i - 1] + 1] + [len(cols)]
    return jnp.concatenate([w[:, int(cols[a]):int(cols[b - 1]) + 1] for a, b in zip(cuts[:-1], cuts[1:])], axis=1)


def kernel(x, p, ffn1_norm, ffn1_w_gate, ffn1_w_up, ffn1_w_down, mix_norm, w_in, conv_qk, b_mlstm_gates, b_fox_f, mlstm_out_norm, fox_out_norm, w_out, ffn2_norm, ffn2_w_gate, ffn2_w_up, ffn2_w_down, ple_gate_norm, w_ple_gate, w_ple_proj, ple_proj_norm, final_norm):
    depth = w_in.shape[0]
    assert depth == 1, "the final RMSNorm is fused into the (single) layer's last kernel"
    assert x.shape[1] % TM_OUT == 0 and x.shape[1] % (L_M * MLSTM_CHUNKS) == 0 and x.shape[2] == D_MODEL
    assert TM_IN % UNIT == 0 and TM_OUT % UNIT == 0 and UNIT % L_M == 0 and FOX_UNROLL % 2 == 0
    row = lambda v: v.reshape(1, -1).astype(F32)
    col = lambda v: jnp.broadcast_to(v.astype(F32)[:, None], (v.shape[0], LANES))
    ucum = jnp.asarray(_tri(UNIT, UNIT), BF16)
    ubd = jnp.asarray(_tri(UNIT, L_M), BF16)
    e48 = jnp.asarray(_placement(), BF16)
    h = x
    for i in range(depth):
        def scaled(w, g):
            return (g.astype(F32)[:, None] * w).astype(BF16)

        wnat = scaled(_take_cols(w_in[i], _NAT_COLS), mix_norm[i])
        wt = scaled(_take_cols(w_in[i], _T_COLS), mix_norm[i]).T
        conv = conv_qk[i].astype(F32)
        bias16 = col(jnp.concatenate([b_mlstm_gates[i], b_fox_f[i]]))
        h1, mq, mk, mo, mvt, mg, fq, fk, fvt = _mix_in(
            h, ffn1_w_gate[i], ffn1_w_up[i], ffn1_w_down[i], col(ffn1_norm[i]), wnat, wt, conv, bias16, ucum, ubd, e48)
        ym = _mlstm(mq, mk, mvt, mg, mo)
        yf = _fox(fq, fk, fvt)
        wo = scaled(w_out[i], jnp.concatenate([mlstm_out_norm[i], fox_out_norm[i]]))
        h = _mix_out(
            h1, ym, yf, p[i], wo[:D_MV], wo[D_MV:], scaled(ffn2_w_gate[i], ffn2_norm[i]),
            scaled(ffn2_w_up[i], ffn2_norm[i]), (0.5 * ffn2_w_down[i]).astype(BF16),
            scaled(w_ple_gate[i], ple_gate_norm[i]), w_ple_proj[i].astype(BF16), row(ple_proj_norm[i]),
            row(final_norm))
    return h
```
